```python
import jax, jax.numpy as jnp
from jax import lax
import numpy as np

D_MODEL = 1024
BATCH = 32
SEQ = 2048
DEPTH = 1

N_META = 16
Q_BLOCK = 128
EPS = 1e-6
FOX_HEADS = 8
FOX_HEAD_DIM = 64
FOX_W = FOX_HEADS * FOX_HEAD_DIM
MLA_HEADS = 8
MLA_Q_RANK = 256
MLA_KV_RANK = 128
MLA_NOPE_DIM = 64
MLA_ROPE_DIM = 32
MLA_QK_DIM = MLA_NOPE_DIM + MLA_ROPE_DIM
MLA_V_DIM = 64
MLA_W = MLA_HEADS * MLA_V_DIM
ROPE_THETA = 10000.0
D_FF = 2816
N_BRANCH = 2
IN_SIZES = (FOX_W, FOX_W, FOX_W, FOX_HEADS, MLA_Q_RANK, MLA_KV_RANK, MLA_ROPE_DIM, N_BRANCH * D_MODEL)
IN_W = FOX_W * 3 + FOX_HEADS + MLA_Q_RANK + MLA_KV_RANK + MLA_ROPE_DIM + N_BRANCH * D_MODEL

kernel_name = "hybrid_fox_mla_macaron_meta"


def rms_norm(x, gain):
    xf = x.astype(jnp.float32)
    y = xf * lax.rsqrt(jnp.mean(xf * xf, axis=-1, keepdims=True) + EPS)
    return (y * gain.astype(jnp.float32)).astype(x.dtype)


def swiglu_ffn(u, w_gu, w_down):
    g, up = jnp.split(u @ w_gu, 2, axis=-1)
    return (jax.nn.silu(g) * up) @ w_down


def rope(x, cos, sin):
    xf = x.astype(jnp.float32)
    x1, x2 = jnp.split(xf, 2, axis=-1)
    out = jnp.concatenate([x1 * cos - x2 * sin, x1 * sin + x2 * cos], axis=-1)
    return out.astype(x.dtype)


def block_causal_attention(q, k, v, log_forget_cum=None):
    L = q.shape[2]
    scale = q.shape[-1] ** -0.5
    bounds = [(0, N_META)] + [(N_META + i * Q_BLOCK, N_META + (i + 1) * Q_BLOCK)
                              for i in range((L - N_META) // Q_BLOCK)]
    outs = []
    for qs, qe in bounds:
        qb = q[:, :, qs:qe].astype(jnp.float32)
        kb = k[:, :, :qe].astype(jnp.float32)
        s = jnp.einsum("bhqd,bhkd->bhqk", qb, kb) * scale
        if log_forget_cum is not None:
            s = s + log_forget_cum[:, :, qs:qe, None] - log_forget_cum[:, :, None, :qe]
        mask = jnp.arange(qs, qe)[:, None] >= jnp.arange(qe)[None, :]
        p = jax.nn.softmax(jnp.where(mask, s, -jnp.inf), axis=-1)
        outs.append(jnp.einsum("bhqk,bhkd->bhqd", p.astype(v.dtype), v[:, :, :qe]))
    return jnp.concatenate(outs, axis=2)


def split_heads(t, n_heads):
    B, L, _ = t.shape
    return t.reshape(B, L, n_heads, -1).transpose(0, 2, 1, 3)


def merge_heads(t):
    B, H, L, dh = t.shape
    return t.transpose(0, 2, 1, 3).reshape(B, L, H * dh)


def token_mixing(u, w_in, b_forget, b_gate, fox_q_norm, fox_k_norm, mla_cq_norm, mla_w_uq,
                 mla_ckv_norm, mla_w_ukv, mla_q_norm, mla_k_norm, w_branch_fox, w_branch_mla,
                 w_out):
    B, L, _ = u.shape
    cuts = list(np.cumsum(IN_SIZES)[:-1])
    fq, fk, fv, f_logit, c_q, c_kv, k_rope, gate_logit = jnp.split(u @ w_in, cuts, axis=-1)

    fq = rms_norm(split_heads(fq, FOX_HEADS), fox_q_norm)
    fk = rms_norm(split_heads(fk, FOX_HEADS), fox_k_norm)
    fv = split_heads(fv, FOX_HEADS)
    log_f = jax.nn.log_sigmoid(f_logit.astype(jnp.float32) + b_forget.astype(jnp.float32))
    log_f_cum = jnp.cumsum(log_f.transpose(0, 2, 1), axis=-1)
    o_fox = merge_heads(block_causal_attention(fq, fk, fv, log_f_cum)) @ w_branch_fox

    q = split_heads(rms_norm(c_q, mla_cq_norm) @ mla_w_uq, MLA_HEADS)
    kv = split_heads(rms_norm(c_kv, mla_ckv_norm) @ mla_w_ukv, MLA_HEADS)
    k_nope, v = jnp.split(kv, [MLA_NOPE_DIM], axis=-1)
    k_rope_h = jnp.broadcast_to(k_rope[:, None], (B, MLA_HEADS, L, MLA_ROPE_DIM))
    k = jnp.concatenate([k_nope, k_rope_h], axis=-1)
    q = rms_norm(q, mla_q_norm)
    k = rms_norm(k, mla_k_norm)
    pos = jnp.arange(L, dtype=jnp.float32)
    inv_freq = ROPE_THETA ** (-jnp.arange(0, MLA_ROPE_DIM, 2, dtype=jnp.float32) / MLA_ROPE_DIM)
    ang = pos[:, None] * inv_freq[None, :]
    cos, sin = jnp.cos(ang), jnp.sin(ang)
    q = jnp.concatenate([q[..., :MLA_NOPE_DIM], rope(q[..., MLA_NOPE_DIM:], cos, sin)], axis=-1)
    k = jnp.concatenate([k[..., :MLA_NOPE_DIM], rope(k[..., MLA_NOPE_DIM:], cos, sin)], axis=-1)
    o_mla = merge_heads(block_causal_attention(q, k, v)) @ w_branch_mla

    g_fox, g_mla = jnp.split(jax.nn.sigmoid(gate_logit + b_gate), N_BRANCH, axis=-1)
    return (g_fox * o_fox + g_mla * o_mla) @ w_out


def setup_inputs(seed: int = 0) -> dict:
    key = jax.random.key(seed)
    ks = iter(jax.random.split(key, 32))

    def w(shape, fan_in):
        return jax.random.normal(next(ks), shape, jnp.float32) * fan_in ** -0.5

    def gain(shape):
        return 1.0 + 0.1 * jax.random.normal(next(ks), shape, jnp.float32)

    def bias(shape, scale=0.1, center=0.0):
        return center + scale * jax.random.normal(next(ks), shape, jnp.float32)

    Dp = DEPTH
    return {
        "x": jax.random.normal(next(ks), (BATCH, SEQ, D_MODEL), jnp.float32),
        "meta_tokens": jax.random.normal(next(ks), (N_META, D_MODEL), jnp.float32),
        "ffn1_norm": gain((Dp, D_MODEL)),
        "ffn1_w_gu": w((Dp, D_MODEL, 2 * D_FF), D_MODEL),
        "ffn1_w_down": w((Dp, D_FF, D_MODEL), D_FF),
        "mix_norm": gain((Dp, D_MODEL)),
        "w_in": w((Dp, D_MODEL, IN_W), D_MODEL),
        "b_forget": bias((Dp, FOX_HEADS), 0.1, 2.0),
        "b_gate": bias((Dp, N_BRANCH * D_MODEL)),
        "fox_q_norm": gain((Dp, FOX_HEAD_DIM)),
        "fox_k_norm": gain((Dp, FOX_HEAD_DIM)),
        "mla_cq_norm": gain((Dp, MLA_Q_RANK)),
        "mla_w_uq": w((Dp, MLA_Q_RANK, MLA_HEADS * MLA_QK_DIM), MLA_Q_RANK),
        "mla_ckv_norm": gain((Dp, MLA_KV_RANK)),
        "mla_w_ukv": w((Dp, MLA_KV_RANK, MLA_HEADS * (MLA_NOPE_DIM + MLA_V_DIM)), MLA_KV_RANK),
        "mla_q_norm": gain((Dp, MLA_QK_DIM)),
        "mla_k_norm": gain((Dp, MLA_QK_DIM)),
        "w_branch_fox": w((Dp, FOX_W, D_MODEL), FOX_W),
        "w_branch_mla": w((Dp, MLA_W, D_MODEL), MLA_W),
        "w_out": w((Dp, D_MODEL, D_MODEL), D_MODEL),
        "ffn2_norm": gain((Dp, D_MODEL)),
        "ffn2_w_gu": w((Dp, D_MODEL, 2 * D_FF), D_MODEL),
        "ffn2_w_down": w((Dp, D_FF, D_MODEL), D_FF),
    }


def reference(x, meta_tokens, ffn1_norm, ffn1_w_gu, ffn1_w_down, mix_norm, w_in, b_forget, b_gate,
              fox_q_norm, fox_k_norm, mla_cq_norm, mla_w_uq, mla_ckv_norm, mla_w_ukv, mla_q_norm,
              mla_k_norm, w_branch_fox, w_branch_mla, w_out, ffn2_norm, ffn2_w_gu, ffn2_w_down):
    B = x.shape[0]
    meta = jnp.broadcast_to(meta_tokens.astype(x.dtype)[None], (B, N_META, D_MODEL))
    h = jnp.concatenate([meta, x], axis=1)
    for l in range(DEPTH):
        h = h + 0.5 * swiglu_ffn(rms_norm(h, ffn1_norm[l]), ffn1_w_gu[l], ffn1_w_down[l])
        h = h + token_mixing(rms_norm(h, mix_norm[l]), w_in[l], b_forget[l], b_gate[l],
                             fox_q_norm[l], fox_k_norm[l], mla_cq_norm[l], mla_w_uq[l],
                             mla_ckv_norm[l], mla_w_ukv[l], mla_q_norm[l], mla_k_norm[l],
                             w_branch_fox[l], w_branch_mla[l], w_out[l])
        h = h + 0.5 * swiglu_ffn(rms_norm(h, ffn2_norm[l]), ffn2_w_gu[l], ffn2_w_down[l])
    return h[:, N_META:]
```

```python
import functools

import jax
import jax.numpy as jnp
from jax import lax
from jax.experimental import pallas as pl
from jax.experimental.pallas import tpu as pltpu

F32 = jnp.float32
BF16 = jnp.bfloat16

EPS = 1e-6
N_META = 16
FOX_HEADS = 8
FOX_DIM = 64
FOX_W = FOX_HEADS * FOX_DIM
MLA_HEADS = 8
MLA_Q_RANK = 256
MLA_KV_RANK = 128
MLA_NOPE = 64
MLA_ROPE = 32
MLA_QK = MLA_NOPE + MLA_ROPE
MLA_V = 64
ROPE_THETA = 10000.0
N_BRANCH = 2
LOG2E = 1.4426950408889634

N_HEADS = FOX_HEADS + MLA_HEADS
HEAD_PAD = 128
V_DIM = 64

OFF_FQ = 0
OFF_FK = OFF_FQ + FOX_W
OFF_FV = OFF_FK + FOX_W
OFF_FL = OFF_FV + FOX_W
OFF_CQ = OFF_FL + FOX_HEADS
OFF_CKV = OFF_CQ + MLA_Q_RANK
OFF_KR = OFF_CKV + MLA_KV_RANK
OFF_GATE = OFF_KR + MLA_ROPE

V7X_VMEM_LIMIT_BYTES = 56 * 1024 * 1024
MASK_VALUE = -1e30

_NT = (((1,), (1,)), ((), ()))


def _const_spec(shape):
    zeros = (0,) * len(shape)
    return pl.BlockSpec(shape, lambda *_: zeros, pipeline_mode=pl.Buffered(1))


def _ffn_kernel(x_ref, gain_ref, wgu_ref, wd_ref, o_ref, *, n_chunks):
    x = x_ref[...]
    d_ff = wd_ref.shape[0]
    ck = d_ff // n_chunks
    ms = jnp.mean(x * x, axis=-1, keepdims=True)
    u = (x * lax.rsqrt(ms + EPS) * gain_ref[...]).astype(BF16)
    acc = None
    for c in range(n_chunks):
        g = jnp.dot(u, wgu_ref[:, c * ck:(c + 1) * ck], preferred_element_type=F32)
        up = jnp.dot(u, wgu_ref[:, d_ff + c * ck:d_ff + (c + 1) * ck], preferred_element_type=F32)
        a = (g * jax.nn.sigmoid(g) * up).astype(BF16)
        d = jnp.dot(a, wd_ref[c * ck:(c + 1) * ck, :], preferred_element_type=F32)
        acc = d if acc is None else acc + d
    o_ref[...] = x + 0.5 * acc


def _ffn(x2d, gain, w_gu, w_down, *, tm, n_chunks=2):
    n, d = x2d.shape
    d_ff = w_down.shape[0]
    return pl.pallas_call(
        functools.partial(_ffn_kernel, n_chunks=n_chunks),
        out_shape=jax.ShapeDtypeStruct((n, d), F32),
        grid=(n // tm,),
        in_specs=[
            pl.BlockSpec((tm, d), lambda i: (i, 0)),
            _const_spec((1, d)),
            _const_spec((d, 2 * d_ff)),
            _const_spec((d_ff, d)),
        ],
        out_specs=pl.BlockSpec((tm, d), lambda i: (i, 0)),
        compiler_params=pltpu.CompilerParams(
            dimension_semantics=("arbitrary",), vmem_limit_bytes=V7X_VMEM_LIMIT_BYTES),
        name="ffn",
    )(x2d, gain, w_gu, w_down)


def _bf16_parts(c):
    hi = c.astype(BF16).astype(F32)
    r = c - hi
    mid = r.astype(BF16).astype(F32)
    lo = (r - mid).astype(BF16).astype(F32)
    return hi, mid, lo


def _rows8(vals, tm):
    row = lax.broadcasted_iota(jnp.int32, (8, tm), 0)
    out = jnp.zeros((8, tm), F32)
    for j, v in enumerate(vals):
        out = jnp.where(row == j, v, out)
    return out


def _rms_rows(x, n):
    return lax.rsqrt(jnp.sum(x * x, axis=0, keepdims=True) * (1.0 / n) + EPS)


def _rope_rows(x, cos, sin):
    x1 = x[:MLA_ROPE // 2]
    x2 = x[MLA_ROPE // 2:]
    return x1 * cos - x2 * sin, x1 * sin + x2 * cos


def _proj_kernel(h_ref, gain_ref, wT_ref, bf_ref, bg_ref, gq_ref, gk_ref, gcq_ref, wuqT_ref, gckv_ref,
                 wukvT_ref, gmq_ref, gmk_ref, cos_ref, sin_ref,
                 qT_ref, k_ref, vT_ref, gT_ref, carry_ref, *, meta):
    tm = h_ref.shape[1]
    h = h_ref[0]
    ms = jnp.mean(h * h, axis=-1, keepdims=True)
    u = (h * lax.rsqrt(ms + EPS) * gain_ref[...]).astype(BF16)
    pT = lax.dot_general(wT_ref[...], u, _NT, preferred_element_type=F32)

    fl = pT[OFF_FL:OFF_FL + FOX_HEADS] + bf_ref[...]
    lf = jnp.minimum(fl, 0.0) - jnp.log1p(jnp.exp(-jnp.abs(fl)))
    r_i = lax.broadcasted_iota(jnp.int32, (tm, tm), 0)
    c_i = lax.broadcasted_iota(jnp.int32, (tm, tm), 1)
    upper = jnp.where(r_i <= c_i, 1.0, 0.0).astype(BF16)
    cs = None
    for part in _bf16_parts(lf):
        d = jnp.dot(part.astype(BF16), upper, preferred_element_type=F32)
        cs = d if cs is None else cs + d
    if meta:
        c = cs - cs[:, N_META - 1:N_META]
    else:
        @pl.when(pl.program_id(1) == 0)
        def _():
            carry_ref[...] = jnp.zeros_like(carry_ref)
        carry = carry_ref[...]
        c = cs + carry[:, 0:1]
        carry_ref[...] = carry + cs[:, tm - 1:tm]
    c = c * LOG2E

    ones = jnp.ones((1, tm), F32)
    one_rows = _rows8([ones, ones, ones], tm)
    zpad_fox = jnp.zeros((HEAD_PAD - FOX_DIM - 16, tm), F32)
    zpad_mla = jnp.zeros((HEAD_PAD - MLA_QK, tm), F32)
    gq = gq_ref[...]
    gk = gk_ref[...]
    for hh in range(FOX_HEADS):
        qh = pT[OFF_FQ + hh * FOX_DIM:OFF_FQ + (hh + 1) * FOX_DIM]
        kh = pT[OFF_FK + hh * FOX_DIM:OFF_FK + (hh + 1) * FOX_DIM]
        qn = qh * (_rms_rows(qh, FOX_DIM) * (FOX_DIM ** -0.5 * LOG2E)) * gq
        kn = kh * _rms_rows(kh, FOX_DIM) * gk
        c_rows = _rows8(list(_bf16_parts(c[hh:hh + 1])), tm)
        q_ext = jnp.concatenate([qn, c_rows, one_rows, zpad_fox], axis=0)
        k_ext = jnp.concatenate([kn, one_rows, -c_rows, zpad_fox], axis=0)
        qT_ref[0, hh * HEAD_PAD:(hh + 1) * HEAD_PAD, :] = q_ext.astype(BF16)
        k_ref[0, :, hh * HEAD_PAD:(hh + 1) * HEAD_PAD] = k_ext.T.astype(BF16)
        vT_ref[0, hh * V_DIM:(hh + 1) * V_DIM, :] = (
            pT[OFF_FV + hh * FOX_DIM:OFF_FV + (hh + 1) * FOX_DIM].astype(BF16))

    cq = pT[OFF_CQ:OFF_CQ + MLA_Q_RANK]
    cqn = (cq * _rms_rows(cq, MLA_Q_RANK) * gcq_ref[...]).astype(BF16)
    qm = jnp.dot(wuqT_ref[...], cqn, preferred_element_type=F32)
    ckv = pT[OFF_CKV:OFF_CKV + MLA_KV_RANK]
    ckvn = (ckv * _rms_rows(ckv, MLA_KV_RANK) * gckv_ref[...]).astype(BF16)
    kv = jnp.dot(wukvT_ref[...], ckvn, preferred_element_type=F32)
    kr = pT[OFF_KR:OFF_KR + MLA_ROPE]
    ss_kr = jnp.sum(kr * kr, axis=0, keepdims=True)
    cos = cos_ref[...]
    sin = sin_ref[...]
    gmq = gmq_ref[...]
    gmk = gmk_ref[...]
    for hh in range(MLA_HEADS):
        qh = qm[hh * MLA_QK:(hh + 1) * MLA_QK]
        qn = qh * (_rms_rows(qh, MLA_QK) * (MLA_QK ** -0.5 * LOG2E)) * gmq
        q1, q2 = _rope_rows(qn[MLA_NOPE:], cos, sin)
        q_ext = jnp.concatenate([qn[:MLA_NOPE], q1, q2, zpad_mla], axis=0)
        kn_raw = kv[hh * (MLA_NOPE + MLA_V):hh * (MLA_NOPE + MLA_V) + MLA_NOPE]
        rk = lax.rsqrt((jnp.sum(kn_raw * kn_raw, axis=0, keepdims=True) + ss_kr) * (1.0 / MLA_QK) + EPS)
        kn = kn_raw * rk * gmk[:MLA_NOPE]
        k1, k2 = _rope_rows(kr * rk * gmk[MLA_NOPE:], cos, sin)
        k_ext = jnp.concatenate([kn, k1, k2, zpad_mla], axis=0)
        g = FOX_HEADS + hh
        qT_ref[0, g * HEAD_PAD:(g + 1) * HEAD_PAD, :] = q_ext.astype(BF16)
        k_ref[0, :, g * HEAD_PAD:(g + 1) * HEAD_PAD] = k_ext.T.astype(BF16)
        vT_ref[0, g * V_DIM:(g + 1) * V_DIM, :] = (
            kv[hh * (MLA_NOPE + MLA_V) + MLA_NOPE:(hh + 1) * (MLA_NOPE + MLA_V)].astype(BF16))

    gT_ref[0] = jax.nn.sigmoid(pT[OFF_GATE:] + bg_ref[...]).astype(BF16)


def _proj(h3d, cosT, sinT, p, *, tm, meta):
    b, s, d = h3d.shape
    in_w = p["w_inT"].shape[0]
    n_gate = in_w - OFF_GATE
    consts = [p["mix_norm"], p["w_inT"], p["b_forget"], p["b_gate"], p["fox_q_norm"], p["fox_k_norm"],
              p["mla_cq_norm"], p["mla_w_uqT"], p["mla_ckv_norm"], p["mla_w_ukvT"], p["mla_q_norm"],
              p["mla_k_norm"]]
    return pl.pallas_call(
        functools.partial(_proj_kernel, meta=meta),
        out_shape=(
            jax.ShapeDtypeStruct((b, N_HEADS * HEAD_PAD, s), BF16),
            jax.ShapeDtypeStruct((b, s, N_HEADS * HEAD_PAD), BF16),
            jax.ShapeDtypeStruct((b, N_HEADS * V_DIM, s), BF16),
            jax.ShapeDtypeStruct((b, n_gate, s), BF16),
        ),
        grid=(b, s // tm),
        in_specs=[pl.BlockSpec((1, tm, d), lambda i, t: (i, t, 0))]
        + [_const_spec(c.shape) for c in consts]
        + [pl.BlockSpec((MLA_ROPE // 2, tm), lambda i, t: (0, t))] * 2,
        out_specs=(
            pl.BlockSpec((1, N_HEADS * HEAD_PAD, tm), lambda i, t: (i, 0, t)),
            pl.BlockSpec((1, tm, N_HEADS * HEAD_PAD), lambda i, t: (i, t, 0)),
            pl.BlockSpec((1, N_HEADS * V_DIM, tm), lambda i, t: (i, 0, t)),
            pl.BlockSpec((1, n_gate, tm), lambda i, t: (i, 0, t)),
        ),
        scratch_shapes=[pltpu.VMEM((FOX_HEADS, 128), F32)],
        compiler_params=pltpu.CompilerParams(
            dimension_semantics=("arbitrary", "arbitrary"), vmem_limit_bytes=V7X_VMEM_LIMIT_BYTES),
        name="proj_meta" if meta else "proj",
    )(h3d, *consts, cosT, sinT)


def _attn_kernel(qT_ref, k_ref, vT_ref, km_ref, vmT_ref, o_ref, *, tq):
    s_len = qT_ref.shape[2]
    km = km_ref[...]
    vmT = vmT_ref[...]
    row = lax.broadcasted_iota(jnp.int32, (tq, tq), 0)
    col = lax.broadcasted_iota(jnp.int32, (tq, tq), 1)
    causal = row <= col

    for i in range(s_len // tq):
        qT = qT_ref[0, :, i * tq:(i + 1) * tq]
        s = jnp.dot(km, qT, preferred_element_type=F32)
        m = jnp.max(s, axis=0, keepdims=True)
        p = jnp.exp2(s - m)
        l = jnp.sum(p, axis=0, keepdims=True)
        acc = jnp.dot(vmT, p.astype(BF16), preferred_element_type=F32)

        def step(kt, carry, masked):
            m, l, acc = carry
            off = pl.multiple_of(kt * tq, tq)
            s = jnp.dot(k_ref[0, pl.ds(off, tq), :], qT, preferred_element_type=F32)
            if masked:
                s = jnp.where(causal, s, MASK_VALUE)
            m_new = jnp.maximum(m, jnp.max(s, axis=0, keepdims=True))
            p = jnp.exp2(s - m_new)
            alpha = jnp.exp2(m - m_new)
            l = alpha * l + jnp.sum(p, axis=0, keepdims=True)
            acc = alpha * acc + jnp.dot(vT_ref[0, :, pl.ds(off, tq)], p.astype(BF16),
                                        preferred_element_type=F32)
            return m_new, l, acc

        carry = (m, l, acc)
        if i > 0:
            carry = lax.fori_loop(0, i, functools.partial(step, masked=False), carry)
        m, l, acc = step(i, carry, True)
        o_ref[0, :, i * tq:(i + 1) * tq] = (acc * (1.0 / l)).astype(BF16)


def _attention(qT, k, vT, k_meta, vT_meta, *, tq):
    b, _, s = qT.shape
    return pl.pallas_call(
        functools.partial(_attn_kernel, tq=tq),
        out_shape=jax.ShapeDtypeStruct((b, N_HEADS * V_DIM, s), BF16),
        grid=(b, N_HEADS),
        in_specs=[
            pl.BlockSpec((1, HEAD_PAD, s), lambda i, h: (i, h, 0)),
            pl.BlockSpec((1, s, HEAD_PAD), lambda i, h: (i, 0, h)),
            pl.BlockSpec((1, V_DIM, s), lambda i, h: (i, h, 0)),
            pl.BlockSpec((N_META, HEAD_PAD), lambda i, h: (0, h)),
            pl.BlockSpec((V_DIM, N_META), lambda i, h: (h, 0)),
        ],
        out_specs=pl.BlockSpec((1, V_DIM, s), lambda i, h: (i, h, 0)),
        compiler_params=pltpu.CompilerParams(
            dimension_semantics=("arbitrary", "arbitrary"), vmem_limit_bytes=V7X_VMEM_LIMIT_BYTES),
        name="attention",
    )(qT, k, vT, k_meta, vT_meta)


def _out_kernel(oT_ref, gT_ref, h_ref, wbfT_ref, wbmT_ref, woT_ref, o_ref):
    oT = oT_ref[0]
    d = woT_ref.shape[0]
    yf = jnp.dot(wbfT_ref[...], oT[:FOX_W], preferred_element_type=F32)
    ym = jnp.dot(wbmT_ref[...], oT[FOX_W:], preferred_element_type=F32)
    g = gT_ref[0].astype(F32)
    z = (g[:d] * yf + g[d:] * ym).astype(BF16)
    mixT = jnp.dot(woT_ref[...], z, preferred_element_type=F32)
    o_ref[0] = h_ref[0] + mixT.T


def _out(oT, gT, h3d, wbfT, wbmT, woT, *, tm):
    b, s, d = h3d.shape
    return pl.pallas_call(
        _out_kernel,
        out_shape=jax.ShapeDtypeStruct((b, s, d), F32),
        grid=(b, s // tm),
        in_specs=[
            pl.BlockSpec((1, oT.shape[1], tm), lambda i, t: (i, 0, t)),
            pl.BlockSpec((1, gT.shape[1], tm), lambda i, t: (i, 0, t)),
            pl.BlockSpec((1, tm, d), lambda i, t: (i, t, 0)),
            _const_spec(wbfT.shape),
            _const_spec(wbmT.shape),
            _const_spec(woT.shape),
        ],
        out_specs=pl.BlockSpec((1, tm, d), lambda i, t: (i, t, 0)),
        compiler_params=pltpu.CompilerParams(
            dimension_semantics=("arbitrary", "arbitrary"), vmem_limit_bytes=V7X_VMEM_LIMIT_BYTES),
        name="out_proj",
    )(oT, gT, h3d, wbfT, wbmT, woT)


def _col(v):
    return v.astype(F32).reshape(-1, 1)


def _pick_tile(n, pref):
    t = min(n, pref)
    while n % t:
        t //= 2
    return t


def kernel(x, meta_tokens, ffn1_norm, ffn1_w_gu, ffn1_w_down, mix_norm, w_in, b_forget, b_gate, fox_q_norm,
           fox_k_norm, mla_cq_norm, mla_w_uq, mla_ckv_norm, mla_w_ukv, mla_q_norm, mla_k_norm, w_branch_fox,
           w_branch_mla, w_out, ffn2_norm, ffn2_w_gu, ffn2_w_down):
    b, s, d = x.shape
    depth = ffn1_norm.shape[0]
    meta_rows = 128

    pos = jnp.arange(N_META + s, dtype=F32)
    inv_freq = ROPE_THETA ** (-jnp.arange(0, MLA_ROPE, 2, dtype=F32) / MLA_ROPE)
    ang = pos[:, None] * inv_freq[None, :]
    cosT, sinT = jnp.cos(ang).T, jnp.sin(ang).T
    pad = ((0, 0), (0, meta_rows - N_META))
    cos_meta, sin_meta = jnp.pad(cosT[:, :N_META], pad), jnp.pad(sinT[:, :N_META], pad)
    cos_real, sin_real = cosT[:, N_META:], sinT[:, N_META:]

    tm_ffn = _pick_tile(b * s, 512)
    tm_proj = _pick_tile(s, 256)
    tq = _pick_tile(s, 256)

    h = x.astype(F32)
    hm = jnp.pad(meta_tokens.astype(F32), ((0, meta_rows - N_META), (0, 0)))
    for l in range(depth):
        w1gu, w1d = ffn1_w_gu[l].astype(BF16), ffn1_w_down[l].astype(BF16)
        w2gu, w2d = ffn2_w_gu[l].astype(BF16), ffn2_w_down[l].astype(BF16)
        g1, g2 = ffn1_norm[l].reshape(1, d).astype(F32), ffn2_norm[l].reshape(1, d).astype(F32)
        p = {
            "mix_norm": mix_norm[l].reshape(1, d).astype(F32),
            "w_inT": w_in[l].T.astype(BF16),
            "b_forget": _col(b_forget[l]),
            "b_gate": _col(b_gate[l]),
            "fox_q_norm": _col(fox_q_norm[l]),
            "fox_k_norm": _col(fox_k_norm[l]),
            "mla_cq_norm": _col(mla_cq_norm[l]),
            "mla_w_uqT": mla_w_uq[l].T.astype(BF16),
            "mla_ckv_norm": _col(mla_ckv_norm[l]),
            "mla_w_ukvT": mla_w_ukv[l].T.astype(BF16),
            "mla_q_norm": _col(mla_q_norm[l]),
            "mla_k_norm": _col(mla_k_norm[l]),
        }
        wbfT = w_branch_fox[l].T.astype(BF16)
        wbmT = w_branch_mla[l].T.astype(BF16)
        woT = w_out[l].T.astype(BF16)

        hm1 = _ffn(hm, g1, w1gu, w1d, tm=meta_rows)
        _, km, vmT, _ = _proj(hm1[None], cos_meta, sin_meta, p, tm=meta_rows, meta=True)
        k_meta, vT_meta = km[0, :N_META], vmT[0, :, :N_META]

        h1 = _ffn(h.reshape(b * s, d), g1, w1gu, w1d, tm=tm_ffn).reshape(b, s, d)
        qT, k, vT, gT = _proj(h1, cos_real, sin_real, p, tm=tm_proj, meta=False)
        oT = _attention(qT, k, vT, k_meta, vT_meta, tq=tq)
        h2 = _out(oT, gT, h1, wbfT, wbmT, woT, tm=tm_proj)
        h = _ffn(h2.reshape(b * s, d), g2, w2gu, w2d, tm=tm_ffn).reshape(b, s, d)
        if l + 1 < depth:
            raise NotImplementedError("only depth 1 is supported")
    return h.astype(x.dtype)
```

```python
import functools

import jax
import jax.numpy as jnp
from jax import lax
from jax.experimental import pallas as pl
from jax.experimental.pallas import tpu as pltpu

F32 = jnp.float32
BF16 = jnp.bfloat16

EPS = 1e-6
N_META = 16
FOX_HEADS = 8
FOX_DIM = 64
FOX_W = FOX_HEADS * FOX_DIM
MLA_HEADS = 8
MLA_Q_RANK = 256
MLA_KV_RANK = 128
MLA_NOPE = 64
MLA_ROPE = 32
MLA_QK = MLA_NOPE + MLA_ROPE
MLA_V = 64
ROPE_THETA = 10000.0
N_BRANCH = 2
LOG2E = 1.4426950408889634

N_HEADS = FOX_HEADS + MLA_HEADS
HEAD_PAD = 128
V_DIM = 64

OFF_FQ = 0
OFF_FK = OFF_FQ + FOX_W
OFF_FV = OFF_FK + FOX_W
OFF_FL = OFF_FV + FOX_W
OFF_CQ = OFF_FL + FOX_HEADS
OFF_CKV = OFF_CQ + MLA_Q_RANK
OFF_KR = OFF_CKV + MLA_KV_RANK
OFF_GATE = OFF_KR + MLA_ROPE

V7X_VMEM_LIMIT_BYTES = 56 * 1024 * 1024
MASK_VALUE = -1e30
ATTN_GROUP = 4
ATTN_LOOKAHEAD = 3

_NT = (((1,), (1,)), ((), ()))


def _const_spec(shape):
    zeros = (0,) * len(shape)
    return pl.BlockSpec(shape, lambda *_: zeros, pipeline_mode=pl.Buffered(1))


def _ffn_kernel(x_ref, gain_ref, wgu_ref, wd_ref, o_ref, *, n_chunks):
    x = x_ref[...]
    d_ff = wd_ref.shape[0]
    ck = d_ff // n_chunks
    ms = jnp.mean(x * x, axis=-1, keepdims=True)
    u = (x * lax.rsqrt(ms + EPS) * gain_ref[...]).astype(BF16)
    acc = None
    for c in range(n_chunks):
        g = jnp.dot(u, wgu_ref[:, c * ck:(c + 1) * ck], preferred_element_type=F32)
        up = jnp.dot(u, wgu_ref[:, d_ff + c * ck:d_ff + (c + 1) * ck], preferred_element_type=F32)
        a = (g * jax.nn.sigmoid(g) * up).astype(BF16)
        d = jnp.dot(a, wd_ref[c * ck:(c + 1) * ck, :], preferred_element_type=F32)
        acc = d if acc is None else acc + d
    o_ref[...] = x + 0.5 * acc


def _ffn(x2d, gain, w_gu, w_down, *, tm, n_chunks=2):
    n, d = x2d.shape
    d_ff = w_down.shape[0]
    return pl.pallas_call(
        functools.partial(_ffn_kernel, n_chunks=n_chunks),
        out_shape=jax.ShapeDtypeStruct((n, d), F32),
        grid=(n // tm,),
        in_specs=[
            pl.BlockSpec((tm, d), lambda i: (i, 0)),
            _const_spec((1, d)),
            _const_spec((d, 2 * d_ff)),
            _const_spec((d_ff, d)),
        ],
        out_specs=pl.BlockSpec((tm, d), lambda i: (i, 0)),
        compiler_params=pltpu.CompilerParams(
            dimension_semantics=("arbitrary",), vmem_limit_bytes=V7X_VMEM_LIMIT_BYTES),
        name="ffn",
    )(x2d, gain, w_gu, w_down)


def _bf16_parts(c):
    hi = c.astype(BF16).astype(F32)
    r = c - hi
    mid = r.astype(BF16).astype(F32)
    lo = (r - mid).astype(BF16).astype(F32)
    return hi, mid, lo


def _rows8(vals, tm):
    row = lax.broadcasted_iota(jnp.int32, (8, tm), 0)
    out = jnp.zeros((8, tm), F32)
    for j, v in enumerate(vals):
        out = jnp.where(row == j, v, out)
    return out


def _rms_rows(x, n):
    return lax.rsqrt(jnp.sum(x * x, axis=0, keepdims=True) * (1.0 / n) + EPS)


def _rope_rows(x, cos, sin):
    x1 = x[:MLA_ROPE // 2]
    x2 = x[MLA_ROPE // 2:]
    return x1 * cos - x2 * sin, x1 * sin + x2 * cos


def _proj_kernel(h_ref, gain_ref, wT_ref, bf_ref, bg_ref, gq_ref, gk_ref, gcq_ref, wuqT_ref, gckv_ref,
                 wukvT_ref, gmq_ref, gmk_ref, cos_ref, sin_ref,
                 qT_ref, k_ref, vT_ref, gT_ref, carry_ref, *, meta):
    tm = h_ref.shape[1]
    h = h_ref[0]
    ms = jnp.mean(h * h, axis=-1, keepdims=True)
    u = (h * lax.rsqrt(ms + EPS) * gain_ref[...]).astype(BF16)
    pT = lax.dot_general(wT_ref[...], u, _NT, preferred_element_type=F32)

    fl = pT[OFF_FL:OFF_FL + FOX_HEADS] + bf_ref[...]
    lf = jnp.minimum(fl, 0.0) - jnp.log1p(jnp.exp(-jnp.abs(fl)))
    r_i = lax.broadcasted_iota(jnp.int32, (tm, tm), 0)
    c_i = lax.broadcasted_iota(jnp.int32, (tm, tm), 1)
    upper = jnp.where(r_i <= c_i, 1.0, 0.0).astype(BF16)
    cs = None
    for part in _bf16_parts(lf):
        d = jnp.dot(part.astype(BF16), upper, preferred_element_type=F32)
        cs = d if cs is None else cs + d
    if meta:
        c = cs - cs[:, N_META - 1:N_META]
    else:
        @pl.when(pl.program_id(1) == 0)
        def _():
            carry_ref[...] = jnp.zeros_like(carry_ref)
        carry = carry_ref[...]
        c = cs + carry[:, 0:1]
        carry_ref[...] = carry + cs[:, tm - 1:tm]
    c = c * LOG2E

    ones = jnp.ones((1, tm), F32)
    one_rows = _rows8([ones, ones, ones], tm)
    zpad_fox = jnp.zeros((HEAD_PAD - FOX_DIM - 16, tm), F32)
    zpad_mla = jnp.zeros((HEAD_PAD - MLA_QK, tm), F32)
    gq = gq_ref[...]
    gk = gk_ref[...]
    for hh in range(FOX_HEADS):
        qh = pT[OFF_FQ + hh * FOX_DIM:OFF_FQ + (hh + 1) * FOX_DIM]
        kh = pT[OFF_FK + hh * FOX_DIM:OFF_FK + (hh + 1) * FOX_DIM]
        qn = qh * (_rms_rows(qh, FOX_DIM) * (FOX_DIM ** -0.5 * LOG2E)) * gq
        kn = kh * _rms_rows(kh, FOX_DIM) * gk
        c_rows = _rows8(list(_bf16_parts(c[hh:hh + 1])), tm)
        q_ext = jnp.concatenate([qn, c_rows, one_rows, zpad_fox], axis=0)
        k_ext = jnp.concatenate([kn, one_rows, -c_rows, zpad_fox], axis=0)
        qT_ref[0, hh * HEAD_PAD:(hh + 1) * HEAD_PAD, :] = q_ext.astype(BF16)
        k_ref[0, :, hh * HEAD_PAD:(hh + 1) * HEAD_PAD] = k_ext.T.astype(BF16)
        vT_ref[0, hh * V_DIM:(hh + 1) * V_DIM, :] = (
            pT[OFF_FV + hh * FOX_DIM:OFF_FV + (hh + 1) * FOX_DIM].astype(BF16))

    cq = pT[OFF_CQ:OFF_CQ + MLA_Q_RANK]
    cqn = (cq * _rms_rows(cq, MLA_Q_RANK) * gcq_ref[...]).astype(BF16)
    qm = jnp.dot(wuqT_ref[...], cqn, preferred_element_type=F32)
    ckv = pT[OFF_CKV:OFF_CKV + MLA_KV_RANK]
    ckvn = (ckv * _rms_rows(ckv, MLA_KV_RANK) * gckv_ref[...]).astype(BF16)
    kv = jnp.dot(wukvT_ref[...], ckvn, preferred_element_type=F32)
    kr = pT[OFF_KR:OFF_KR + MLA_ROPE]
    ss_kr = jnp.sum(kr * kr, axis=0, keepdims=True)
    cos = cos_ref[...]
    sin = sin_ref[...]
    gmq = gmq_ref[...]
    gmk = gmk_ref[...]
    for hh in range(MLA_HEADS):
        qh = qm[hh * MLA_QK:(hh + 1) * MLA_QK]
        qn = qh * (_rms_rows(qh, MLA_QK) * (MLA_QK ** -0.5 * LOG2E)) * gmq
        q1, q2 = _rope_rows(qn[MLA_NOPE:], cos, sin)
        q_ext = jnp.concatenate([qn[:MLA_NOPE], q1, q2, zpad_mla], axis=0)
        kn_raw = kv[hh * (MLA_NOPE + MLA_V):hh * (MLA_NOPE + MLA_V) + MLA_NOPE]
        rk = lax.rsqrt((jnp.sum(kn_raw * kn_raw, axis=0, keepdims=True) + ss_kr) * (1.0 / MLA_QK) + EPS)
        kn = kn_raw * rk * gmk[:MLA_NOPE]
        k1, k2 = _rope_rows(kr * rk * gmk[MLA_NOPE:], cos, sin)
        k_ext = jnp.concatenate([kn, k1, k2, zpad_mla], axis=0)
        g = FOX_HEADS + hh
        qT_ref[0, g * HEAD_PAD:(g + 1) * HEAD_PAD, :] = q_ext.astype(BF16)
        k_ref[0, :, g * HEAD_PAD:(g + 1) * HEAD_PAD] = k_ext.T.astype(BF16)
        vT_ref[0, g * V_DIM:(g + 1) * V_DIM, :] = (
            kv[hh * (MLA_NOPE + MLA_V) + MLA_NOPE:(hh + 1) * (MLA_NOPE + MLA_V)].astype(BF16))

    gT_ref[0] = jax.nn.sigmoid(pT[OFF_GATE:] + bg_ref[...]).astype(BF16)


def _proj(h3d, cosT, sinT, p, *, tm, meta):
    b, s, d = h3d.shape
    in_w = p["w_inT"].shape[0]
    n_gate = in_w - OFF_GATE
    consts = [p["mix_norm"], p["w_inT"], p["b_forget"], p["b_gate"], p["fox_q_norm"], p["fox_k_norm"],
              p["mla_cq_norm"], p["mla_w_uqT"], p["mla_ckv_norm"], p["mla_w_ukvT"], p["mla_q_norm"],
              p["mla_k_norm"]]
    return pl.pallas_call(
        functools.partial(_proj_kernel, meta=meta),
        out_shape=(
            jax.ShapeDtypeStruct((b, N_HEADS * HEAD_PAD, s), BF16),
            jax.ShapeDtypeStruct((b, s, N_HEADS * HEAD_PAD), BF16),
            jax.ShapeDtypeStruct((b, N_HEADS * V_DIM, s), BF16),
            jax.ShapeDtypeStruct((b, n_gate, s), BF16),
        ),
        grid=(b, s // tm),
        in_specs=[pl.BlockSpec((1, tm, d), lambda i, t: (i, t, 0))]
        + [_const_spec(c.shape) for c in consts]
        + [pl.BlockSpec((MLA_ROPE // 2, tm), lambda i, t: (0, t))] * 2,
        out_specs=(
            pl.BlockSpec((1, N_HEADS * HEAD_PAD, tm), lambda i, t: (i, 0, t)),
            pl.BlockSpec((1, tm, N_HEADS * HEAD_PAD), lambda i, t: (i, t, 0)),
            pl.BlockSpec((1, N_HEADS * V_DIM, tm), lambda i, t: (i, 0, t)),
            pl.BlockSpec((1, n_gate, tm), lambda i, t: (i, 0, t)),
        ),
        scratch_shapes=[pltpu.VMEM((FOX_HEADS, 128), F32)],
        compiler_params=pltpu.CompilerParams(
            dimension_semantics=("arbitrary", "arbitrary"), vmem_limit_bytes=V7X_VMEM_LIMIT_BYTES),
        name="proj_meta" if meta else "proj",
    )(h3d, *consts, cosT, sinT)


def _attn_kernel(qT_ref, k_ref, vT_ref, km_ref, vmT_ref, o_ref, *, tq, group):
    s_len = qT_ref.shape[2]
    row = lax.broadcasted_iota(jnp.int32, (tq, tq), 0)
    col = lax.broadcasted_iota(jnp.int32, (tq, tq), 1)
    causal = row <= col
    ones_real = jnp.ones((16, s_len), BF16)
    ones_meta = jnp.ones((16, N_META), BF16)
    v_ext = [jnp.concatenate([vT_ref[0, g * V_DIM:(g + 1) * V_DIM, :], ones_real], axis=0)
             for g in range(group)]
    vm_ext = [jnp.concatenate([vmT_ref[g * V_DIM:(g + 1) * V_DIM, :], ones_meta], axis=0)
              for g in range(group)]

    def scores(i, g):
        qT = qT_ref[0, g * HEAD_PAD:(g + 1) * HEAD_PAD, i * tq:(i + 1) * tq]
        n = (i + 1) * tq
        s = jnp.dot(k_ref[0, :n, g * HEAD_PAD:(g + 1) * HEAD_PAD], qT, preferred_element_type=F32)
        sm = jnp.dot(km_ref[:, g * HEAD_PAD:(g + 1) * HEAD_PAD], qT, preferred_element_type=F32)
        return s, sm

    def finish(i, g, s, sm):
        n = (i + 1) * tq
        diag = jnp.where(causal, s[n - tq:], MASK_VALUE)
        m = jnp.maximum(jnp.max(diag, axis=0, keepdims=True), jnp.max(sm, axis=0, keepdims=True))
        parts = [jnp.exp2(diag - m).astype(BF16)]
        if i > 0:
            m = jnp.maximum(m, jnp.max(s[:n - tq], axis=0, keepdims=True))
            parts = [jnp.exp2(s[:n - tq] - m).astype(BF16), jnp.exp2(diag - m).astype(BF16)]
        p = jnp.concatenate(parts, axis=0) if len(parts) > 1 else parts[0]
        pm = jnp.exp2(sm - m).astype(BF16)
        o = (jnp.dot(v_ext[g][:, :n], p, preferred_element_type=F32)
             + jnp.dot(vm_ext[g], pm, preferred_element_type=F32))
        out = o[:V_DIM] * (1.0 / o[V_DIM:V_DIM + 1])
        o_ref[0, g * V_DIM:(g + 1) * V_DIM, i * tq:(i + 1) * tq] = out.astype(BF16)

    pending = []
    for i in range(s_len // tq):
        for g in range(group):
            pending.append((i, g) + scores(i, g))
            if len(pending) > ATTN_LOOKAHEAD:
                finish(*pending.pop(0))
    for unit in pending:
        finish(*unit)


def _attention(qT, k, vT, k_meta, vT_meta, *, tq, group):
    b, _, s = qT.shape
    return pl.pallas_call(
        functools.partial(_attn_kernel, tq=tq, group=group),
        out_shape=jax.ShapeDtypeStruct((b, N_HEADS * V_DIM, s), BF16),
        grid=(b, N_HEADS // group),
        in_specs=[
            pl.BlockSpec((1, group * HEAD_PAD, s), lambda i, h: (i, h, 0)),
            pl.BlockSpec((1, s, group * HEAD_PAD), lambda i, h: (i, 0, h)),
            pl.BlockSpec((1, group * V_DIM, s), lambda i, h: (i, h, 0)),
            pl.BlockSpec((N_META, group * HEAD_PAD), lambda i, h: (0, h)),
            pl.BlockSpec((group * V_DIM, N_META), lambda i, h: (h, 0)),
        ],
        out_specs=pl.BlockSpec((1, group * V_DIM, s), lambda i, h: (i, h, 0)),
        compiler_params=pltpu.CompilerParams(
            dimension_semantics=("arbitrary", "arbitrary"), vmem_limit_bytes=V7X_VMEM_LIMIT_BYTES),
        name="attention",
    )(qT, k, vT, k_meta, vT_meta)


def _out_kernel(oT_ref, gT_ref, h_ref, wbfT_ref, wbmT_ref, woT_ref, o_ref):
    oT = oT_ref[0]
    d = woT_ref.shape[0]
    yf = jnp.dot(wbfT_ref[...], oT[:FOX_W], preferred_element_type=F32)
    ym = jnp.dot(wbmT_ref[...], oT[FOX_W:], preferred_element_type=F32)
    g = gT_ref[0].astype(F32)
    z = (g[:d] * yf + g[d:] * ym).astype(BF16)
    mixT = jnp.dot(woT_ref[...], z, preferred_element_type=F32)
    o_ref[0] = h_ref[0] + mixT.T


def _out(oT, gT, h3d, wbfT, wbmT, woT, *, tm):
    b, s, d = h3d.shape
    return pl.pallas_call(
        _out_kernel,
        out_shape=jax.ShapeDtypeStruct((b, s, d), F32),
        grid=(b, s // tm),
        in_specs=[
            pl.BlockSpec((1, oT.shape[1], tm), lambda i, t: (i, 0, t)),
            pl.BlockSpec((1, gT.shape[1], tm), lambda i, t: (i, 0, t)),
            pl.BlockSpec((1, tm, d), lambda i, t: (i, t, 0)),
            _const_spec(wbfT.shape),
            _const_spec(wbmT.shape),
            _const_spec(woT.shape),
        ],
        out_specs=pl.BlockSpec((1, tm, d), lambda i, t: (i, t, 0)),
        compiler_params=pltpu.CompilerParams(
            dimension_semantics=("arbitrary", "arbitrary"), vmem_limit_bytes=V7X_VMEM_LIMIT_BYTES),
        name="out_proj",
    )(oT, gT, h3d, wbfT, wbmT, woT)


def _col(v):
    return v.astype(F32).reshape(-1, 1)


def _pick_tile(n, pref):
    t = min(n, pref)
    while n % t:
        t //= 2
    return t


def kernel(x, meta_tokens, ffn1_norm, ffn1_w_gu, ffn1_w_down, mix_norm, w_in, b_forget, b_gate, fox_q_norm,
           fox_k_norm, mla_cq_norm, mla_w_uq, mla_ckv_norm, mla_w_ukv, mla_q_norm, mla_k_norm, w_branch_fox,
           w_branch_mla, w_out, ffn2_norm, ffn2_w_gu, ffn2_w_down):
    b, s, d = x.shape
    depth = ffn1_norm.shape[0]
    meta_rows = 128

    pos = jnp.arange(N_META + s, dtype=F32)
    inv_freq = ROPE_THETA ** (-jnp.arange(0, MLA_ROPE, 2, dtype=F32) / MLA_ROPE)
    ang = pos[:, None] * inv_freq[None, :]
    cosT, sinT = jnp.cos(ang).T, jnp.sin(ang).T
    pad = ((0, 0), (0, meta_rows - N_META))
    cos_meta, sin_meta = jnp.pad(cosT[:, :N_META], pad), jnp.pad(sinT[:, :N_META], pad)
    cos_real, sin_real = cosT[:, N_META:], sinT[:, N_META:]

    tm_ffn = _pick_tile(b * s, 512)
    tm_proj = _pick_tile(s, 256)
    tq = _pick_tile(s, 256)

    h = x.astype(F32)
    hm = jnp.pad(meta_tokens.astype(F32), ((0, meta_rows - N_META), (0, 0)))
    for l in range(depth):
        w1gu, w1d = ffn1_w_gu[l].astype(BF16), ffn1_w_down[l].astype(BF16)
        w2gu, w2d = ffn2_w_gu[l].astype(BF16), ffn2_w_down[l].astype(BF16)
        g1, g2 = ffn1_norm[l].reshape(1, d).astype(F32), ffn2_norm[l].reshape(1, d).astype(F32)
        p = {
            "mix_norm": mix_norm[l].reshape(1, d).astype(F32),
            "w_inT": w_in[l].T.astype(BF16),
            "b_forget": _col(b_forget[l]),
            "b_gate": _col(b_gate[l]),
            "fox_q_norm": _col(fox_q_norm[l]),
            "fox_k_norm": _col(fox_k_norm[l]),
            "mla_cq_norm": _col(mla_cq_norm[l]),
            "mla_w_uqT": mla_w_uq[l].T.astype(BF16),
            "mla_ckv_norm": _col(mla_ckv_norm[l]),
            "mla_w_ukvT": mla_w_ukv[l].T.astype(BF16),
            "mla_q_norm": _col(mla_q_norm[l]),
            "mla_k_norm": _col(mla_k_norm[l]),
        }
        wbfT = w_branch_fox[l].T.astype(BF16)
        wbmT = w_branch_mla[l].T.astype(BF16)
        woT = w_out[l].T.astype(BF16)

        hm1 = _ffn(hm, g1, w1gu, w1d, tm=meta_rows)
        _, km, vmT, _ = _proj(hm1[None], cos_meta, sin_meta, p, tm=meta_rows, meta=True)
        k_meta, vT_meta = km[0, :N_META], vmT[0, :, :N_META]

        h1 = _ffn(h.reshape(b * s, d), g1, w1gu, w1d, tm=tm_ffn).reshape(b, s, d)
        qT, k, vT, gT = _proj(h1, cos_real, sin_real, p, tm=tm_proj, meta=False)
        oT = _attention(qT, k, vT, k_meta, vT_meta, tq=tq, group=ATTN_GROUP)
        h2 = _out(oT, gT, h1, wbfT, wbmT, woT, tm=tm_proj)
        h = _ffn(h2.reshape(b * s, d), g2, w2gu, w2d, tm=tm_ffn).reshape(b, s, d)
        if l + 1 < depth:
            raise NotImplementedError("only depth 1 is supported")
    return h.astype(x.dtype)
```

```python
import functools

import jax
import jax.numpy as jnp
from jax import lax
from jax.experimental import pallas as pl
from jax.experimental.pallas import tpu as pltpu

F32 = jnp.float32
BF16 = jnp.bfloat16

EPS = 1e-6
N_META = 16
FOX_HEADS = 8
FOX_DIM = 64
FOX_W = FOX_HEADS * FOX_DIM
MLA_HEADS = 8
MLA_Q_RANK = 256
MLA_KV_RANK = 128
MLA_NOPE = 64
MLA_ROPE = 32
MLA_QK = MLA_NOPE + MLA_ROPE
MLA_V = 64
ROPE_THETA = 10000.0
N_BRANCH = 2
LOG2E = 1.4426950408889634

N_HEADS = FOX_HEADS + MLA_HEADS
HEAD_PAD = 128
V_DIM = 64

OFF_FQ = 0
OFF_FK = OFF_FQ + FOX_W
OFF_FV = OFF_FK + FOX_W
OFF_FL = OFF_FV + FOX_W
OFF_CQ = OFF_FL + FOX_HEADS
OFF_CKV = OFF_CQ + MLA_Q_RANK
OFF_KR = OFF_CKV + MLA_KV_RANK
OFF_GATE = OFF_KR + MLA_ROPE

V7X_VMEM_LIMIT_BYTES = 56 * 1024 * 1024
MASK_VALUE = -1e30
ATTN_GROUP = 4
ATTN_LOOKAHEAD = 3

_NT = (((1,), (1,)), ((), ()))


def _const_spec(shape):
    zeros = (0,) * len(shape)
    return pl.BlockSpec(shape, lambda *_: zeros, pipeline_mode=pl.Buffered(1))


def _ffn_kernel(x_ref, gain_ref, wgu_ref, wd_ref, o_ref, *, n_chunks):
    x = x_ref[...]
    d_ff = wd_ref.shape[0]
    ck = d_ff // n_chunks
    ms = jnp.mean(x * x, axis=-1, keepdims=True)
    u = (x * lax.rsqrt(ms + EPS) * gain_ref[...]).astype(BF16)
    acc = None
    for c in range(n_chunks):
        g = jnp.dot(u, wgu_ref[:, c * ck:(c + 1) * ck], preferred_element_type=F32)
        up = jnp.dot(u, wgu_ref[:, d_ff + c * ck:d_ff + (c + 1) * ck], preferred_element_type=F32)
        a = (g * jax.nn.sigmoid(g) * up).astype(BF16)
        d = jnp.dot(a, wd_ref[c * ck:(c + 1) * ck, :], preferred_element_type=F32)
        acc = d if acc is None else acc + d
    o_ref[...] = x + 0.5 * acc


def _ffn(x2d, gain, w_gu, w_down, *, tm, n_chunks=1):
    n, d = x2d.shape
    d_ff = w_down.shape[0]
    return pl.pallas_call(
        functools.partial(_ffn_kernel, n_chunks=n_chunks),
        out_shape=jax.ShapeDtypeStruct((n, d), F32),
        grid=(n // tm,),
        in_specs=[
            pl.BlockSpec((tm, d), lambda i: (i, 0)),
            _const_spec((1, d)),
            _const_spec((d, 2 * d_ff)),
            _const_spec((d_ff, d)),
        ],
        out_specs=pl.BlockSpec((tm, d), lambda i: (i, 0)),
        compiler_params=pltpu.CompilerParams(
            dimension_semantics=("arbitrary",), vmem_limit_bytes=V7X_VMEM_LIMIT_BYTES),
        name="ffn",
    )(x2d, gain, w_gu, w_down)


def _bf16_parts(c):
    hi = c.astype(BF16).astype(F32)
    r = c - hi
    mid = r.astype(BF16).astype(F32)
    lo = (r - mid).astype(BF16).astype(F32)
    return hi, mid, lo


def _rows8(vals, tm):
    row = lax.broadcasted_iota(jnp.int32, (8, tm), 0)
    out = jnp.zeros((8, tm), F32)
    for j, v in enumerate(vals):
        out = jnp.where(row == j, v, out)
    return out


def _rms_rows(x, n):
    return lax.rsqrt(jnp.sum(x * x, axis=0, keepdims=True) * (1.0 / n) + EPS)


def _rope_rows(x, cos, sin):
    x1 = x[:MLA_ROPE // 2]
    x2 = x[MLA_ROPE // 2:]
    return x1 * cos - x2 * sin, x1 * sin + x2 * cos


def _proj_kernel(h_ref, gain_ref, wT_ref, bf_ref, bg_ref, gq_ref, gk_ref, gcq_ref, wuqT_ref, gckv_ref,
                 wukvT_ref, gmq_ref, gmk_ref, cos_ref, sin_ref,
                 qT_ref, k_ref, vT_ref, gT_ref, carry_ref, *, meta, sub):
    tm = h_ref.shape[1]
    r_i = lax.broadcasted_iota(jnp.int32, (sub, sub), 0)
    c_i = lax.broadcasted_iota(jnp.int32, (sub, sub), 1)
    upper = jnp.where(r_i <= c_i, 1.0, 0.0).astype(BF16)
    ones = jnp.ones((1, sub), F32)
    one_rows = _rows8([ones, ones, ones], sub)
    zpad_fox = jnp.zeros((HEAD_PAD - FOX_DIM - 16, sub), F32)
    zpad_mla = jnp.zeros((HEAD_PAD - MLA_QK, sub), F32)
    gq = gq_ref[...]
    gk = gk_ref[...]
    gmq = gmq_ref[...]
    gmk = gmk_ref[...]

    def project(j):
        h = h_ref[0, j * sub:(j + 1) * sub, :]
        ms = jnp.mean(h * h, axis=-1, keepdims=True)
        u = (h * lax.rsqrt(ms + EPS) * gain_ref[...]).astype(BF16)
        return lax.dot_general(wT_ref[...], u, _NT, preferred_element_type=F32)

    def finish(j, pT, carry):
        lanes = slice(j * sub, (j + 1) * sub)
        fl = pT[OFF_FL:OFF_FL + FOX_HEADS] + bf_ref[...]
        lf = jnp.minimum(fl, 0.0) - jnp.log1p(jnp.exp(-jnp.abs(fl)))
        cs = None
        for part in _bf16_parts(lf):
            d = jnp.dot(part.astype(BF16), upper, preferred_element_type=F32)
            cs = d if cs is None else cs + d
        if meta:
            c = cs - cs[:, N_META - 1:N_META]
        else:
            c = cs + carry
            carry = carry + cs[:, sub - 1:sub]
        c = c * LOG2E

        for hh in range(FOX_HEADS):
            qh = pT[OFF_FQ + hh * FOX_DIM:OFF_FQ + (hh + 1) * FOX_DIM]
            kh = pT[OFF_FK + hh * FOX_DIM:OFF_FK + (hh + 1) * FOX_DIM]
            qn = qh * (_rms_rows(qh, FOX_DIM) * (FOX_DIM ** -0.5 * LOG2E)) * gq
            kn = kh * _rms_rows(kh, FOX_DIM) * gk
            c_rows = _rows8(list(_bf16_parts(c[hh:hh + 1])), sub)
            q_ext = jnp.concatenate([qn, c_rows, one_rows, zpad_fox], axis=0)
            k_ext = jnp.concatenate([kn, one_rows, -c_rows, zpad_fox], axis=0)
            qT_ref[0, hh * HEAD_PAD:(hh + 1) * HEAD_PAD, lanes] = q_ext.astype(BF16)
            k_ref[0, lanes, hh * HEAD_PAD:(hh + 1) * HEAD_PAD] = k_ext.T.astype(BF16)
            vT_ref[0, hh * V_DIM:(hh + 1) * V_DIM, lanes] = (
                pT[OFF_FV + hh * FOX_DIM:OFF_FV + (hh + 1) * FOX_DIM].astype(BF16))

        cq = pT[OFF_CQ:OFF_CQ + MLA_Q_RANK]
        cqn = (cq * _rms_rows(cq, MLA_Q_RANK) * gcq_ref[...]).astype(BF16)
        qm = jnp.dot(wuqT_ref[...], cqn, preferred_element_type=F32)
        ckv = pT[OFF_CKV:OFF_CKV + MLA_KV_RANK]
        ckvn = (ckv * _rms_rows(ckv, MLA_KV_RANK) * gckv_ref[...]).astype(BF16)
        kv = jnp.dot(wukvT_ref[...], ckvn, preferred_element_type=F32)
        kr = pT[OFF_KR:OFF_KR + MLA_ROPE]
        ss_kr = jnp.sum(kr * kr, axis=0, keepdims=True)
        cos = cos_ref[:, lanes]
        sin = sin_ref[:, lanes]
        for hh in range(MLA_HEADS):
            qh = qm[hh * MLA_QK:(hh + 1) * MLA_QK]
            qn = qh * (_rms_rows(qh, MLA_QK) * (MLA_QK ** -0.5 * LOG2E)) * gmq
            q1, q2 = _rope_rows(qn[MLA_NOPE:], cos, sin)
            q_ext = jnp.concatenate([qn[:MLA_NOPE], q1, q2, zpad_mla], axis=0)
            kn_raw = kv[hh * (MLA_NOPE + MLA_V):hh * (MLA_NOPE + MLA_V) + MLA_NOPE]
            rk = lax.rsqrt((jnp.sum(kn_raw * kn_raw, axis=0, keepdims=True) + ss_kr) * (1.0 / MLA_QK) + EPS)
            kn = kn_raw * rk * gmk[:MLA_NOPE]
            k1, k2 = _rope_rows(kr * rk * gmk[MLA_NOPE:], cos, sin)
            k_ext = jnp.concatenate([kn, k1, k2, zpad_mla], axis=0)
            g = FOX_HEADS + hh
            qT_ref[0, g * HEAD_PAD:(g + 1) * HEAD_PAD, lanes] = q_ext.astype(BF16)
            k_ref[0, lanes, g * HEAD_PAD:(g + 1) * HEAD_PAD] = k_ext.T.astype(BF16)
            vT_ref[0, g * V_DIM:(g + 1) * V_DIM, lanes] = (
                kv[hh * (MLA_NOPE + MLA_V) + MLA_NOPE:(hh + 1) * (MLA_NOPE + MLA_V)].astype(BF16))

        gT_ref[0, :, lanes] = jax.nn.sigmoid(pT[OFF_GATE:] + bg_ref[...]).astype(BF16)
        return carry

    carry = None
    if not meta:
        @pl.when(pl.program_id(1) == 0)
        def _():
            carry_ref[...] = jnp.zeros_like(carry_ref)
        carry = carry_ref[:, 0:1]
    n_sub = tm // sub
    pT = project(0)
    for j in range(n_sub):
        pT_next = project(j + 1) if j + 1 < n_sub else None
        carry = finish(j, pT, carry)
        pT = pT_next
    if not meta:
        carry_ref[...] = jnp.broadcast_to(carry, carry_ref.shape)


def _proj(h3d, cosT, sinT, p, *, tm, sub, meta):
    b, s, d = h3d.shape
    in_w = p["w_inT"].shape[0]
    n_gate = in_w - OFF_GATE
    consts = [p["mix_norm"], p["w_inT"], p["b_forget"], p["b_gate"], p["fox_q_norm"], p["fox_k_norm"],
              p["mla_cq_norm"], p["mla_w_uqT"], p["mla_ckv_norm"], p["mla_w_ukvT"], p["mla_q_norm"],
              p["mla_k_norm"]]
    return pl.pallas_call(
        functools.partial(_proj_kernel, meta=meta, sub=sub),
        out_shape=(
            jax.ShapeDtypeStruct((b, N_HEADS * HEAD_PAD, s), BF16),
            jax.ShapeDtypeStruct((b, s, N_HEADS * HEAD_PAD), BF16),
            jax.ShapeDtypeStruct((b, N_HEADS * V_DIM, s), BF16),
            jax.ShapeDtypeStruct((b, n_gate, s), BF16),
        ),
        grid=(b, s // tm),
        in_specs=[pl.BlockSpec((1, tm, d), lambda i, t: (i, t, 0))]
        + [_const_spec(c.shape) for c in consts]
        + [pl.BlockSpec((MLA_ROPE // 2, tm), lambda i, t: (0, t))] * 2,
        out_specs=(
            pl.BlockSpec((1, N_HEADS * HEAD_PAD, tm), lambda i, t: (i, 0, t)),
            pl.BlockSpec((1, tm, N_HEADS * HEAD_PAD), lambda i, t: (i, t, 0)),
            pl.BlockSpec((1, N_HEADS * V_DIM, tm), lambda i, t: (i, 0, t)),
            pl.BlockSpec((1, n_gate, tm), lambda i, t: (i, 0, t)),
        ),
        scratch_shapes=[pltpu.VMEM((FOX_HEADS, 128), F32)],
        compiler_params=pltpu.CompilerParams(
            dimension_semantics=("arbitrary", "arbitrary"), vmem_limit_bytes=V7X_VMEM_LIMIT_BYTES),
        name="proj_meta" if meta else "proj",
    )(h3d, *consts, cosT, sinT)


def _attn_kernel(qT_ref, k_ref, vT_ref, km_ref, vmT_ref, o_ref, *, tq, group):
    s_len = qT_ref.shape[2]
    row = lax.broadcasted_iota(jnp.int32, (tq, tq), 0)
    col = lax.broadcasted_iota(jnp.int32, (tq, tq), 1)
    causal = row <= col
    ones_real = jnp.ones((16, s_len), BF16)
    ones_meta = jnp.ones((16, N_META), BF16)
    v_ext = [jnp.concatenate([vT_ref[0, g * V_DIM:(g + 1) * V_DIM, :], ones_real], axis=0)
             for g in range(group)]
    vm_ext = [jnp.concatenate([vmT_ref[g * V_DIM:(g + 1) * V_DIM, :], ones_meta], axis=0)
              for g in range(group)]

    def scores(i, g):
        qT = qT_ref[0, g * HEAD_PAD:(g + 1) * HEAD_PAD, i * tq:(i + 1) * tq]
        n = (i + 1) * tq
        s = jnp.dot(k_ref[0, :n, g * HEAD_PAD:(g + 1) * HEAD_PAD], qT, preferred_element_type=F32)
        sm = jnp.dot(km_ref[:, g * HEAD_PAD:(g + 1) * HEAD_PAD], qT, preferred_element_type=F32)
        return s, sm

    def finish(i, g, s, sm):
        n = (i + 1) * tq
        diag = jnp.where(causal, s[n - tq:], MASK_VALUE)
        m = jnp.maximum(jnp.max(diag, axis=0, keepdims=True), jnp.max(sm, axis=0, keepdims=True))
        parts = [jnp.exp2(diag - m).astype(BF16)]
        if i > 0:
            m = jnp.maximum(m, jnp.max(s[:n - tq], axis=0, keepdims=True))
            parts = [jnp.exp2(s[:n - tq] - m).astype(BF16), jnp.exp2(diag - m).astype(BF16)]
        p = jnp.concatenate(parts, axis=0) if len(parts) > 1 else parts[0]
        pm = jnp.exp2(sm - m).astype(BF16)
        o = (jnp.dot(v_ext[g][:, :n], p, preferred_element_type=F32)
             + jnp.dot(vm_ext[g], pm, preferred_element_type=F32))
        out = o[:V_DIM] * (1.0 / o[V_DIM:V_DIM + 1])
        o_ref[0, g * V_DIM:(g + 1) * V_DIM, i * tq:(i + 1) * tq] = out.astype(BF16)

    pending = []
    for i in range(s_len // tq):
        for g in range(group):
            pending.append((i, g) + scores(i, g))
            if len(pending) > ATTN_LOOKAHEAD:
                finish(*pending.pop(0))
    for unit in pending:
        finish(*unit)


def _attention(qT, k, vT, k_meta, vT_meta, *, tq, group):
    b, _, s = qT.shape
    return pl.pallas_call(
        functools.partial(_attn_kernel, tq=tq, group=group),
        out_shape=jax.ShapeDtypeStruct((b, N_HEADS * V_DIM, s), BF16),
        grid=(b, N_HEADS // group),
        in_specs=[
            pl.BlockSpec((1, group * HEAD_PAD, s), lambda i, h: (i, h, 0)),
            pl.BlockSpec((1, s, group * HEAD_PAD), lambda i, h: (i, 0, h)),
            pl.BlockSpec((1, group * V_DIM, s), lambda i, h: (i, h, 0)),
            pl.BlockSpec((N_META, group * HEAD_PAD), lambda i, h: (0, h)),
            pl.BlockSpec((group * V_DIM, N_META), lambda i, h: (h, 0)),
        ],
        out_specs=pl.BlockSpec((1, group * V_DIM, s), lambda i, h: (i, h, 0)),
        compiler_params=pltpu.CompilerParams(
            dimension_semantics=("arbitrary", "arbitrary"), vmem_limit_bytes=V7X_VMEM_LIMIT_BYTES),
        name="attention",
    )(qT, k, vT, k_meta, vT_meta)


def _out_kernel(oT_ref, gT_ref, h_ref, wbfT_ref, wbmT_ref, woT_ref, o_ref, *, sub):
    tm = h_ref.shape[1]
    d = woT_ref.shape[0]

    def branches(j):
        lanes = slice(j * sub, (j + 1) * sub)
        yf = jnp.dot(wbfT_ref[...], oT_ref[0, :FOX_W, lanes], preferred_element_type=F32)
        ym = jnp.dot(wbmT_ref[...], oT_ref[0, FOX_W:, lanes], preferred_element_type=F32)
        gf = gT_ref[0, :d, lanes].astype(F32)
        gm = gT_ref[0, d:, lanes].astype(F32)
        return (gf * yf + gm * ym).astype(BF16)

    def finish(j, z):
        mixT = jnp.dot(woT_ref[...], z, preferred_element_type=F32)
        rows = slice(j * sub, (j + 1) * sub)
        o_ref[0, rows, :] = h_ref[0, rows, :] + mixT.T

    n_sub = tm // sub
    z = branches(0)
    for j in range(n_sub):
        z_next = branches(j + 1) if j + 1 < n_sub else None
        finish(j, z)
        z = z_next


def _out(oT, gT, h3d, wbfT, wbmT, woT, *, tm, sub):
    b, s, d = h3d.shape
    return pl.pallas_call(
        functools.partial(_out_kernel, sub=sub),
        out_shape=jax.ShapeDtypeStruct((b, s, d), F32),
        grid=(b, s // tm),
        in_specs=[
            pl.BlockSpec((1, oT.shape[1], tm), lambda i, t: (i, 0, t)),
            pl.BlockSpec((1, gT.shape[1], tm), lambda i, t: (i, 0, t)),
            pl.BlockSpec((1, tm, d), lambda i, t: (i, t, 0)),
            _const_spec(wbfT.shape),
            _const_spec(wbmT.shape),
            _const_spec(woT.shape),
        ],
        out_specs=pl.BlockSpec((1, tm, d), lambda i, t: (i, t, 0)),
        compiler_params=pltpu.CompilerParams(
            dimension_semantics=("arbitrary", "arbitrary"), vmem_limit_bytes=V7X_VMEM_LIMIT_BYTES),
        name="out_proj",
    )(oT, gT, h3d, wbfT, wbmT, woT)


def _col(v):
    return v.astype(F32).reshape(-1, 1)


def _pick_tile(n, pref):
    t = min(n, pref)
    while n % t:
        t //= 2
    return t


def kernel(x, meta_tokens, ffn1_norm, ffn1_w_gu, ffn1_w_down, mix_norm, w_in, b_forget, b_gate, fox_q_norm,
           fox_k_norm, mla_cq_norm, mla_w_uq, mla_ckv_norm, mla_w_ukv, mla_q_norm, mla_k_norm, w_branch_fox,
           w_branch_mla, w_out, ffn2_norm, ffn2_w_gu, ffn2_w_down):
    b, s, d = x.shape
    depth = ffn1_norm.shape[0]
    if depth != 1:
        raise NotImplementedError("only depth 1 is supported")
    meta_rows = 128

    pos = jnp.arange(N_META + s, dtype=F32)
    inv_freq = ROPE_THETA ** (-jnp.arange(0, MLA_ROPE, 2, dtype=F32) / MLA_ROPE)
    ang = pos[:, None] * inv_freq[None, :]
    cosT, sinT = jnp.cos(ang).T, jnp.sin(ang).T
    pad = ((0, 0), (0, meta_rows - N_META))
    cos_meta, sin_meta = jnp.pad(cosT[:, :N_META], pad), jnp.pad(sinT[:, :N_META], pad)
    cos_real, sin_real = cosT[:, N_META:], sinT[:, N_META:]

    tm_ffn = _pick_tile(b * s, 512)
    tm_proj = _pick_tile(s, 512)
    sub_proj = _pick_tile(tm_proj, 256)
    tq = _pick_tile(s, 256)

    w1gu, w1d = ffn1_w_gu[0].astype(BF16), ffn1_w_down[0].astype(BF16)
    w2gu, w2d = ffn2_w_gu[0].astype(BF16), ffn2_w_down[0].astype(BF16)
    g1, g2 = ffn1_norm[0].reshape(1, d).astype(F32), ffn2_norm[0].reshape(1, d).astype(F32)
    p = {
        "mix_norm": mix_norm[0].reshape(1, d).astype(F32),
        "w_inT": w_in[0].T.astype(BF16),
        "b_forget": _col(b_forget[0]),
        "b_gate": _col(b_gate[0]),
        "fox_q_norm": _col(fox_q_norm[0]),
        "fox_k_norm": _col(fox_k_norm[0]),
        "mla_cq_norm": _col(mla_cq_norm[0]),
        "mla_w_uqT": mla_w_uq[0].T.astype(BF16),
        "mla_ckv_norm": _col(mla_ckv_norm[0]),
        "mla_w_ukvT": mla_w_ukv[0].T.astype(BF16),
        "mla_q_norm": _col(mla_q_norm[0]),
        "mla_k_norm": _col(mla_k_norm[0]),
    }
    wbfT = w_branch_fox[0].T.astype(BF16)
    wbmT = w_branch_mla[0].T.astype(BF16)
    woT = w_out[0].T.astype(BF16)

    hm = jnp.pad(meta_tokens.astype(F32), ((0, meta_rows - N_META), (0, 0)))
    hm1 = _ffn(hm, g1, w1gu, w1d, tm=meta_rows)
    _, km, vmT, _ = _proj(hm1[None], cos_meta, sin_meta, p, tm=meta_rows, sub=meta_rows, meta=True)
    k_meta, vT_meta = km[0, :N_META], vmT[0, :, :N_META]

    h1 = _ffn(x.astype(F32).reshape(b * s, d), g1, w1gu, w1d, tm=tm_ffn).reshape(b, s, d)
    qT, k, vT, gT = _proj(h1, cos_real, sin_real, p, tm=tm_proj, sub=sub_proj, meta=False)
    oT = _attention(qT, k, vT, k_meta, vT_meta, tq=tq, group=ATTN_GROUP)
    h2 = _out(oT, gT, h1, wbfT, wbmT, woT, tm=tm_proj, sub=sub_proj)
    h3 = _ffn(h2.reshape(b * s, d), g2, w2gu, w2d, tm=tm_ffn).reshape(b, s, d)
    return h3.astype(x.dtype)
```

```python
import functools

import jax
import jax.numpy as jnp
from jax import lax
from jax.experimental import pallas as pl
from jax.experimental.pallas import tpu as pltpu

F32 = jnp.float32
BF16 = jnp.bfloat16

EPS = 1e-6
N_META = 16
FOX_HEADS = 8
FOX_DIM = 64
FOX_W = FOX_HEADS * FOX_DIM
MLA_HEADS = 8
MLA_Q_RANK = 256
MLA_KV_RANK = 128
MLA_NOPE = 64
MLA_ROPE = 32
MLA_QK = MLA_NOPE + MLA_ROPE
MLA_V = 64
ROPE_THETA = 10000.0
N_BRANCH = 2
LOG2E = 1.4426950408889634

N_HEADS = FOX_HEADS + MLA_HEADS
HEAD_PAD = 128
V_DIM = 64

OFF_FQ = 0
OFF_FK = OFF_FQ + FOX_W
OFF_FV = OFF_FK + FOX_W
OFF_FL = OFF_FV + FOX_W
OFF_CQ = OFF_FL + FOX_HEADS
OFF_CKV = OFF_CQ + MLA_Q_RANK
OFF_KR = OFF_CKV + MLA_KV_RANK
OFF_GATE = OFF_KR + MLA_ROPE

V7X_VMEM_LIMIT_BYTES = 56 * 1024 * 1024
MASK_VALUE = -1e30
ATTN_GROUP = 4
ATTN_LOOKAHEAD = 3

_NT = (((1,), (1,)), ((), ()))


def _const_spec(shape):
    zeros = (0,) * len(shape)
    return pl.BlockSpec(shape, lambda *_: zeros, pipeline_mode=pl.Buffered(1))


def _gate_up(x, gain, wgu_ref):
    d_ff = wgu_ref.shape[1] // 2
    ms = jnp.mean(x * x, axis=-1, keepdims=True)
    u = (x * lax.rsqrt(ms + EPS) * gain).astype(BF16)
    g = jnp.dot(u, wgu_ref[:, :d_ff], preferred_element_type=F32)
    up = jnp.dot(u, wgu_ref[:, d_ff:], preferred_element_type=F32)
    return g, up


def _half_ffn_out(x, g, up, wd_ref):
    a = (g * jax.nn.sigmoid(g) * up).astype(BF16)
    return x + 0.5 * jnp.dot(a, wd_ref[...], preferred_element_type=F32)


def _ffn_kernel(x_ref, gain_ref, wgu_ref, wd_ref, o_ref):
    x = x_ref[...]
    g, up = _gate_up(x, gain_ref[...], wgu_ref)
    o_ref[...] = _half_ffn_out(x, g, up, wd_ref)


def _ffn(x2d, gain, w_gu, w_down, *, tm):
    n, d = x2d.shape
    d_ff = w_down.shape[0]
    return pl.pallas_call(
        _ffn_kernel,
        out_shape=jax.ShapeDtypeStruct((n, d), F32),
        grid=(n // tm,),
        in_specs=[
            pl.BlockSpec((tm, d), lambda i: (i, 0)),
            _const_spec((1, d)),
            _const_spec((d, 2 * d_ff)),
            _const_spec((d_ff, d)),
        ],
        out_specs=pl.BlockSpec((tm, d), lambda i: (i, 0)),
        compiler_params=pltpu.CompilerParams(
            dimension_semantics=("arbitrary",), vmem_limit_bytes=V7X_VMEM_LIMIT_BYTES),
        name="ffn",
    )(x2d, gain, w_gu, w_down)


def _bf16_parts(c):
    hi = c.astype(BF16).astype(F32)
    r = c - hi
    mid = r.astype(BF16).astype(F32)
    lo = (r - mid).astype(BF16).astype(F32)
    return hi, mid, lo


def _rows8(vals, tm):
    row = lax.broadcasted_iota(jnp.int32, (8, tm), 0)
    out = jnp.zeros((8, tm), F32)
    for j, v in enumerate(vals):
        out = jnp.where(row == j, v, out)
    return out


def _rms_rows(x, n):
    return lax.rsqrt(jnp.sum(x * x, axis=0, keepdims=True) * (1.0 / n) + EPS)


def _rope_rows(x, cos, sin):
    x1 = x[:MLA_ROPE // 2]
    x2 = x[MLA_ROPE // 2:]
    return x1 * cos - x2 * sin, x1 * sin + x2 * cos


def _proj_kernel(h_ref, gain_ref, wT_ref, bf_ref, bg_ref, gq_ref, gk_ref, gcq_ref, wuqT_ref, gckv_ref,
                 wukvT_ref, gmq_ref, gmk_ref, cos_ref, sin_ref,
                 qT_ref, k_ref, vT_ref, gT_ref, carry_ref, *, meta, sub):
    tm = h_ref.shape[1]
    r_i = lax.broadcasted_iota(jnp.int32, (sub, sub), 0)
    c_i = lax.broadcasted_iota(jnp.int32, (sub, sub), 1)
    upper = jnp.where(r_i <= c_i, 1.0, 0.0).astype(BF16)
    ones = jnp.ones((1, sub), F32)
    one_rows = _rows8([ones, ones, ones], sub)
    zpad_fox = jnp.zeros((HEAD_PAD - FOX_DIM - 16, sub), F32)
    zpad_mla = jnp.zeros((HEAD_PAD - MLA_QK, sub), F32)
    gq = gq_ref[...]
    gk = gk_ref[...]
    gmq = gmq_ref[...]
    gmk = gmk_ref[...]

    def project(j):
        h = h_ref[0, j * sub:(j + 1) * sub, :]
        ms = jnp.mean(h * h, axis=-1, keepdims=True)
        u = (h * lax.rsqrt(ms + EPS) * gain_ref[...]).astype(BF16)
        return lax.dot_general(wT_ref[...], u, _NT, preferred_element_type=F32)

    def finish(j, pT, carry):
        lanes = slice(j * sub, (j + 1) * sub)
        fl = pT[OFF_FL:OFF_FL + FOX_HEADS] + bf_ref[...]
        lf = jnp.minimum(fl, 0.0) - jnp.log1p(jnp.exp(-jnp.abs(fl)))
        cs = None
        for part in _bf16_parts(lf):
            d = jnp.dot(part.astype(BF16), upper, preferred_element_type=F32)
            cs = d if cs is None else cs + d
        if meta:
            c = cs - cs[:, N_META - 1:N_META]
        else:
            c = cs + carry
            carry = carry + cs[:, sub - 1:sub]
        c = c * LOG2E

        for hh in range(FOX_HEADS):
            qh = pT[OFF_FQ + hh * FOX_DIM:OFF_FQ + (hh + 1) * FOX_DIM]
            kh = pT[OFF_FK + hh * FOX_DIM:OFF_FK + (hh + 1) * FOX_DIM]
            qn = qh * (_rms_rows(qh, FOX_DIM) * (FOX_DIM ** -0.5 * LOG2E)) * gq
            kn = kh * _rms_rows(kh, FOX_DIM) * gk
            c_rows = _rows8(list(_bf16_parts(c[hh:hh + 1])), sub)
            q_ext = jnp.concatenate([qn, c_rows, one_rows, zpad_fox], axis=0)
            k_ext = jnp.concatenate([kn, one_rows, -c_rows, zpad_fox], axis=0)
            qT_ref[0, hh * HEAD_PAD:(hh + 1) * HEAD_PAD, lanes] = q_ext.astype(BF16)
            k_ref[0, lanes, hh * HEAD_PAD:(hh + 1) * HEAD_PAD] = k_ext.T.astype(BF16)
            vT_ref[0, hh * V_DIM:(hh + 1) * V_DIM, lanes] = (
                pT[OFF_FV + hh * FOX_DIM:OFF_FV + (hh + 1) * FOX_DIM].astype(BF16))

        cq = pT[OFF_CQ:OFF_CQ + MLA_Q_RANK]
        cqn = (cq * _rms_rows(cq, MLA_Q_RANK) * gcq_ref[...]).astype(BF16)
        qm = jnp.dot(wuqT_ref[...], cqn, preferred_element_type=F32)
        ckv = pT[OFF_CKV:OFF_CKV + MLA_KV_RANK]
        ckvn = (ckv * _rms_rows(ckv, MLA_KV_RANK) * gckv_ref[...]).astype(BF16)
        kv = jnp.dot(wukvT_ref[...], ckvn, preferred_element_type=F32)
        kr = pT[OFF_KR:OFF_KR + MLA_ROPE]
        ss_kr = jnp.sum(kr * kr, axis=0, keepdims=True)
        cos = cos_ref[:, lanes]
        sin = sin_ref[:, lanes]
        for hh in range(MLA_HEADS):
            qh = qm[hh * MLA_QK:(hh + 1) * MLA_QK]
            qn = qh * (_rms_rows(qh, MLA_QK) * (MLA_QK ** -0.5 * LOG2E)) * gmq
            q1, q2 = _rope_rows(qn[MLA_NOPE:], cos, sin)
            q_ext = jnp.concatenate([qn[:MLA_NOPE], q1, q2, zpad_mla], axis=0)
            kn_raw = kv[hh * (MLA_NOPE + MLA_V):hh * (MLA_NOPE + MLA_V) + MLA_NOPE]
            rk = lax.rsqrt((jnp.sum(kn_raw * kn_raw, axis=0, keepdims=True) + ss_kr) * (1.0 / MLA_QK) + EPS)
            kn = kn_raw * rk * gmk[:MLA_NOPE]
            k1, k2 = _rope_rows(kr * rk * gmk[MLA_NOPE:], cos, sin)
            k_ext = jnp.concatenate([kn, k1, k2, zpad_mla], axis=0)
            g = FOX_HEADS + hh
            qT_ref[0, g * HEAD_PAD:(g + 1) * HEAD_PAD, lanes] = q_ext.astype(BF16)
            k_ref[0, lanes, g * HEAD_PAD:(g + 1) * HEAD_PAD] = k_ext.T.astype(BF16)
            vT_ref[0, g * V_DIM:(g + 1) * V_DIM, lanes] = (
                kv[hh * (MLA_NOPE + MLA_V) + MLA_NOPE:(hh + 1) * (MLA_NOPE + MLA_V)].astype(BF16))

        gT_ref[0, :, lanes] = jax.nn.sigmoid(pT[OFF_GATE:] + bg_ref[...]).astype(BF16)
        return carry

    carry = None
    if not meta:
        @pl.when(pl.program_id(1) == 0)
        def _():
            carry_ref[...] = jnp.zeros_like(carry_ref)
        carry = carry_ref[:, 0:1]
    n_sub = tm // sub
    pT = project(0)
    for j in range(n_sub):
        pT_next = project(j + 1) if j + 1 < n_sub else None
        carry = finish(j, pT, carry)
        pT = pT_next
    if not meta:
        carry_ref[...] = jnp.broadcast_to(carry, carry_ref.shape)


def _proj(h3d, cosT, sinT, p, *, tm, sub, meta):
    b, s, d = h3d.shape
    in_w = p["w_inT"].shape[0]
    n_gate = in_w - OFF_GATE
    consts = [p["mix_norm"], p["w_inT"], p["b_forget"], p["b_gate"], p["fox_q_norm"], p["fox_k_norm"],
              p["mla_cq_norm"], p["mla_w_uqT"], p["mla_ckv_norm"], p["mla_w_ukvT"], p["mla_q_norm"],
              p["mla_k_norm"]]
    return pl.pallas_call(
        functools.partial(_proj_kernel, meta=meta, sub=sub),
        out_shape=(
            jax.ShapeDtypeStruct((b, N_HEADS * HEAD_PAD, s), BF16),
            jax.ShapeDtypeStruct((b, s, N_HEADS * HEAD_PAD), BF16),
            jax.ShapeDtypeStruct((b, N_HEADS * V_DIM, s), BF16),
            jax.ShapeDtypeStruct((b, n_gate, s), BF16),
        ),
        grid=(b, s // tm),
        in_specs=[pl.BlockSpec((1, tm, d), lambda i, t: (i, t, 0))]
        + [_const_spec(c.shape) for c in consts]
        + [pl.BlockSpec((MLA_ROPE // 2, tm), lambda i, t: (0, t))] * 2,
        out_specs=(
            pl.BlockSpec((1, N_HEADS * HEAD_PAD, tm), lambda i, t: (i, 0, t)),
            pl.BlockSpec((1, tm, N_HEADS * HEAD_PAD), lambda i, t: (i, t, 0)),
            pl.BlockSpec((1, N_HEADS * V_DIM, tm), lambda i, t: (i, 0, t)),
            pl.BlockSpec((1, n_gate, tm), lambda i, t: (i, 0, t)),
        ),
        scratch_shapes=[pltpu.VMEM((FOX_HEADS, 128), F32)],
        compiler_params=pltpu.CompilerParams(
            dimension_semantics=("arbitrary", "arbitrary"), vmem_limit_bytes=V7X_VMEM_LIMIT_BYTES),
        name="proj_meta" if meta else "proj",
    )(h3d, *consts, cosT, sinT)


def _attn_kernel(qT_ref, k_ref, vT_ref, km_ref, vmT_ref, o_ref, *, tq, group):
    s_len = qT_ref.shape[2]
    row = lax.broadcasted_iota(jnp.int32, (tq, tq), 0)
    col = lax.broadcasted_iota(jnp.int32, (tq, tq), 1)
    causal = row <= col
    ones_real = jnp.ones((16, s_len), BF16)
    ones_meta = jnp.ones((16, N_META), BF16)
    v_ext = [jnp.concatenate([vT_ref[0, g * V_DIM:(g + 1) * V_DIM, :], ones_real], axis=0)
             for g in range(group)]
    vm_ext = [jnp.concatenate([vmT_ref[g * V_DIM:(g + 1) * V_DIM, :], ones_meta], axis=0)
              for g in range(group)]

    def scores(i, g):
        qT = qT_ref[0, g * HEAD_PAD:(g + 1) * HEAD_PAD, i * tq:(i + 1) * tq]
        n = (i + 1) * tq
        s = jnp.dot(k_ref[0, :n, g * HEAD_PAD:(g + 1) * HEAD_PAD], qT, preferred_element_type=F32)
        sm = jnp.dot(km_ref[:, g * HEAD_PAD:(g + 1) * HEAD_PAD], qT, preferred_element_type=F32)
        return s, sm

    def finish(i, g, s, sm):
        n = (i + 1) * tq
        diag = jnp.where(causal, s[n - tq:], MASK_VALUE)
        m = jnp.maximum(jnp.max(diag, axis=0, keepdims=True), jnp.max(sm, axis=0, keepdims=True))
        parts = [jnp.exp2(diag - m).astype(BF16)]
        if i > 0:
            m = jnp.maximum(m, jnp.max(s[:n - tq], axis=0, keepdims=True))
            parts = [jnp.exp2(s[:n - tq] - m).astype(BF16), jnp.exp2(diag - m).astype(BF16)]
        p = jnp.concatenate(parts, axis=0) if len(parts) > 1 else parts[0]
        pm = jnp.exp2(sm - m).astype(BF16)
        o = (jnp.dot(v_ext[g][:, :n], p, preferred_element_type=F32)
             + jnp.dot(vm_ext[g], pm, preferred_element_type=F32))
        out = o[:V_DIM] * (1.0 / o[V_DIM:V_DIM + 1])
        o_ref[0, g * V_DIM:(g + 1) * V_DIM, i * tq:(i + 1) * tq] = out.astype(BF16)

    nq = s_len // tq
    order = [t for pair in zip(range(nq - 1, -1, -1), range(nq)) for t in pair][:nq]
    pending = []
    for i in order:
        for g in range(group):
            pending.append((i, g) + scores(i, g))
            if len(pending) > ATTN_LOOKAHEAD:
                finish(*pending.pop(0))
    for unit in pending:
        finish(*unit)


def _attention(qT, k, vT, k_meta, vT_meta, *, tq, group):
    b, _, s = qT.shape
    return pl.pallas_call(
        functools.partial(_attn_kernel, tq=tq, group=group),
        out_shape=jax.ShapeDtypeStruct((b, N_HEADS * V_DIM, s), BF16),
        grid=(b, N_HEADS // group),
        in_specs=[
            pl.BlockSpec((1, group * HEAD_PAD, s), lambda i, h: (i, h, 0)),
            pl.BlockSpec((1, s, group * HEAD_PAD), lambda i, h: (i, 0, h)),
            pl.BlockSpec((1, group * V_DIM, s), lambda i, h: (i, h, 0)),
            pl.BlockSpec((N_META, group * HEAD_PAD), lambda i, h: (0, h)),
            pl.BlockSpec((group * V_DIM, N_META), lambda i, h: (h, 0)),
        ],
        out_specs=pl.BlockSpec((1, group * V_DIM, s), lambda i, h: (i, h, 0)),
        compiler_params=pltpu.CompilerParams(
            dimension_semantics=("arbitrary", "arbitrary"), vmem_limit_bytes=V7X_VMEM_LIMIT_BYTES),
        name="attention",
    )(qT, k, vT, k_meta, vT_meta)


def _out_ffn_kernel(oT_ref, gT_ref, h_ref, wbfT_ref, wbmT_ref, woT_ref, gain_ref, wgu_ref, wd_ref, o_ref, *, sub):
    tm = h_ref.shape[1]
    d = woT_ref.shape[0]
    n_sub = tm // sub

    def branches(j):
        lanes = slice(j * sub, (j + 1) * sub)
        yf = jnp.dot(wbfT_ref[...], oT_ref[0, :FOX_W, lanes], preferred_element_type=F32)
        ym = jnp.dot(wbmT_ref[...], oT_ref[0, FOX_W:, lanes], preferred_element_type=F32)
        gf = gT_ref[0, :d, lanes].astype(F32)
        gm = gT_ref[0, d:, lanes].astype(F32)
        return (gf * yf + gm * ym).astype(BF16)

    def mixed(j, z):
        mixT = jnp.dot(woT_ref[...], z, preferred_element_type=F32)
        return h_ref[0, j * sub:(j + 1) * sub, :] + mixT.T

    zs = [branches(j) for j in range(n_sub)]
    h2 = [mixed(j, zs[j]) for j in range(n_sub)]
    gu = [_gate_up(h2[j], gain_ref[...], wgu_ref) for j in range(n_sub)]
    for j in range(n_sub):
        o_ref[0, j * sub:(j + 1) * sub, :] = _half_ffn_out(h2[j], *gu[j], wd_ref)


def _out_ffn(oT, gT, h3d, wbfT, wbmT, woT, gain, w_gu, w_down, *, tm, sub):
    b, s, d = h3d.shape
    return pl.pallas_call(
        functools.partial(_out_ffn_kernel, sub=sub),
        out_shape=jax.ShapeDtypeStruct((b, s, d), F32),
        grid=(b, s // tm),
        in_specs=[
            pl.BlockSpec((1, oT.shape[1], tm), lambda i, t: (i, 0, t)),
            pl.BlockSpec((1, gT.shape[1], tm), lambda i, t: (i, 0, t)),
            pl.BlockSpec((1, tm, d), lambda i, t: (i, t, 0)),
            _const_spec(wbfT.shape),
            _const_spec(wbmT.shape),
            _const_spec(woT.shape),
            _const_spec(gain.shape),
            _const_spec(w_gu.shape),
            _const_spec(w_down.shape),
        ],
        out_specs=pl.BlockSpec((1, tm, d), lambda i, t: (i, t, 0)),
        compiler_params=pltpu.CompilerParams(
            dimension_semantics=("arbitrary", "arbitrary"), vmem_limit_bytes=V7X_VMEM_LIMIT_BYTES),
        name="out_ffn",
    )(oT, gT, h3d, wbfT, wbmT, woT, gain, w_gu, w_down)


def _col(v):
    return v.astype(F32).reshape(-1, 1)


def _pick_tile(n, pref):
    t = min(n, pref)
    while n % t:
        t //= 2
    return t


def kernel(x, meta_tokens, ffn1_norm, ffn1_w_gu, ffn1_w_down, mix_norm, w_in, b_forget, b_gate, fox_q_norm,
           fox_k_norm, mla_cq_norm, mla_w_uq, mla_ckv_norm, mla_w_ukv, mla_q_norm, mla_k_norm, w_branch_fox,
           w_branch_mla, w_out, ffn2_norm, ffn2_w_gu, ffn2_w_down):
    b, s, d = x.shape
    depth = ffn1_norm.shape[0]
    if depth != 1:
        raise NotImplementedError("only depth 1 is supported")
    meta_rows = 128

    pos = jnp.arange(N_META + s, dtype=F32)
    inv_freq = ROPE_THETA ** (-jnp.arange(0, MLA_ROPE, 2, dtype=F32) / MLA_ROPE)
    ang = pos[:, None] * inv_freq[None, :]
    cosT, sinT = jnp.cos(ang).T, jnp.sin(ang).T
    pad = ((0, 0), (0, meta_rows - N_META))
    cos_meta, sin_meta = jnp.pad(cosT[:, :N_META], pad), jnp.pad(sinT[:, :N_META], pad)
    cos_real, sin_real = cosT[:, N_META:], sinT[:, N_META:]

    tm_ffn = _pick_tile(b * s, 512)
    tm_proj = _pick_tile(s, 512)
    sub_proj = _pick_tile(tm_proj, 256)
    tq = _pick_tile(s, 256)

    w1gu, w1d = ffn1_w_gu[0].astype(BF16), ffn1_w_down[0].astype(BF16)
    w2gu, w2d = ffn2_w_gu[0].astype(BF16), ffn2_w_down[0].astype(BF16)
    g1, g2 = ffn1_norm[0].reshape(1, d).astype(F32), ffn2_norm[0].reshape(1, d).astype(F32)
    p = {
        "mix_norm": mix_norm[0].reshape(1, d).astype(F32),
        "w_inT": w_in[0].T.astype(BF16),
        "b_forget": _col(b_forget[0]),
        "b_gate": _col(b_gate[0]),
        "fox_q_norm": _col(fox_q_norm[0]),
        "fox_k_norm": _col(fox_k_norm[0]),
        "mla_cq_norm": _col(mla_cq_norm[0]),
        "mla_w_uqT": mla_w_uq[0].T.astype(BF16),
        "mla_ckv_norm": _col(mla_ckv_norm[0]),
        "mla_w_ukvT": mla_w_ukv[0].T.astype(BF16),
        "mla_q_norm": _col(mla_q_norm[0]),
        "mla_k_norm": _col(mla_k_norm[0]),
    }
    wbfT = w_branch_fox[0].T.astype(BF16)
    wbmT = w_branch_mla[0].T.astype(BF16)
    woT = w_out[0].T.astype(BF16)

    hm = jnp.pad(meta_tokens.astype(F32), ((0, meta_rows - N_META), (0, 0)))
    hm1 = _ffn(hm, g1, w1gu, w1d, tm=meta_rows)
    _, km, vmT, _ = _proj(hm1[None], cos_meta, sin_meta, p, tm=meta_rows, sub=meta_rows, meta=True)
    k_meta, vT_meta = km[0, :N_META], vmT[0, :, :N_META]

    h1 = _ffn(x.astype(F32).reshape(b * s, d), g1, w1gu, w1d, tm=tm_ffn).reshape(b, s, d)
    qT, k, vT, gT = _proj(h1, cos_real, sin_real, p, tm=tm_proj, sub=sub_proj, meta=False)
    oT = _attention(qT, k, vT, k_meta, vT_meta, tq=tq, group=ATTN_GROUP)
    h3 = _out_ffn(oT, gT, h1, wbfT, wbmT, woT, g2, w2gu, w2d, tm=tm_proj, sub=sub_proj)
    return h3.astype(x.dtype)
```

```python
import collections
import functools

import jax
import jax.numpy as jnp
from jax import lax
from jax.experimental import pallas as pl
from jax.experimental.pallas import tpu as pltpu

F32 = jnp.float32
BF16 = jnp.bfloat16

EPS = 1e-6
N_META = 16
FOX_HEADS = 8
FOX_DIM = 64
FOX_W = FOX_HEADS * FOX_DIM
MLA_HEADS = 8
MLA_Q_RANK = 256
MLA_KV_RANK = 128
MLA_NOPE = 64
MLA_ROPE = 32
MLA_QK = MLA_NOPE + MLA_ROPE
MLA_V = 64
ROPE_THETA = 10000.0
N_BRANCH = 2
LOG2E = 1.4426950408889634

N_HEADS = FOX_HEADS + MLA_HEADS
HEAD_PAD = 128
V_DIM = 64

OFF_FQ = 0
OFF_FK = OFF_FQ + FOX_W
OFF_FV = OFF_FK + FOX_W
OFF_FL = OFF_FV + FOX_W
OFF_CQ = OFF_FL + FOX_HEADS
OFF_CKV = OFF_CQ + MLA_Q_RANK
OFF_KR = OFF_CKV + MLA_KV_RANK
OFF_GATE = OFF_KR + MLA_ROPE
LOW_FL = 0
LOW_CQ = LOW_FL + FOX_HEADS
LOW_CKV = LOW_CQ + MLA_Q_RANK
LOW_KR = LOW_CKV + MLA_KV_RANK
LOW_ROWS = LOW_KR + MLA_ROPE
WIDE_FQ = 0
WIDE_FK = WIDE_FQ + FOX_W
WIDE_FV = WIDE_FK + FOX_W
WIDE_GATE = WIDE_FV + FOX_W

V7X_VMEM_LIMIT_BYTES = 56 * 1024 * 1024
MASK_VALUE = -1e30
ATTN_GROUP = 4
ATTN_LOOKAHEAD = 3
GATE_SPLIT_EIGHTHS = (0, 2, 4, 7, 8)

_NT = (((1,), (1,)), ((), ()))


def _const_spec(shape):
    zeros = (0,) * len(shape)
    return pl.BlockSpec(shape, lambda *_: zeros, pipeline_mode=pl.Buffered(1))


def _gate_up(x, gain, wgu_ref):
    d_ff = wgu_ref.shape[1] // 2
    ms = jnp.mean(x * x, axis=-1, keepdims=True)
    u = (x * lax.rsqrt(ms + EPS) * gain).astype(BF16)
    g = jnp.dot(u, wgu_ref[:, :d_ff], preferred_element_type=F32)
    up = jnp.dot(u, wgu_ref[:, d_ff:], preferred_element_type=F32)
    return g, up


def _half_ffn_out(x, g, up, wd_ref):
    a = (g * jax.nn.sigmoid(g) * up).astype(BF16)
    return x + 0.5 * jnp.dot(a, wd_ref[...], preferred_element_type=F32)


def _ffn_kernel(x_ref, gain_ref, wgu_ref, wd_ref, o_ref):
    x = x_ref[...]
    g, up = _gate_up(x, gain_ref[...], wgu_ref)
    o_ref[...] = _half_ffn_out(x, g, up, wd_ref)


def _ffn(x2d, gain, w_gu, w_down, *, tm):
    n, d = x2d.shape
    d_ff = w_down.shape[0]
    return pl.pallas_call(
        _ffn_kernel,
        out_shape=jax.ShapeDtypeStruct((n, d), F32),
        grid=(n // tm,),
        in_specs=[
            pl.BlockSpec((tm, d), lambda i: (i, 0)),
            _const_spec((1, d)),
            _const_spec((d, 2 * d_ff)),
            _const_spec((d_ff, d)),
        ],
        out_specs=pl.BlockSpec((tm, d), lambda i: (i, 0)),
        compiler_params=pltpu.CompilerParams(
            dimension_semantics=("arbitrary",), vmem_limit_bytes=V7X_VMEM_LIMIT_BYTES),
        name="ffn",
    )(x2d, gain, w_gu, w_down)


def _bf16_parts(c):
    hi = c.astype(BF16).astype(F32)
    r = c - hi
    mid = r.astype(BF16).astype(F32)
    lo = (r - mid).astype(BF16).astype(F32)
    return hi, mid, lo


def _rows8(vals, tm):
    row = lax.broadcasted_iota(jnp.int32, (8, tm), 0)
    out = jnp.zeros((8, tm), F32)
    for j, v in enumerate(vals):
        out = jnp.where(row == j, v, out)
    return out


def _rms_rows(x, n):
    return lax.rsqrt(jnp.sum(x * x, axis=0, keepdims=True) * (1.0 / n) + EPS)


def _rope_rows(x, cos, sin):
    x1 = x[:MLA_ROPE // 2]
    x2 = x[MLA_ROPE // 2:]
    return x1 * cos - x2 * sin, x1 * sin + x2 * cos


_ProjRefs = collections.namedtuple(
    "_ProjRefs", "gain wlowT wwideT bf bg gq gk gcq wuqT gckv wukvT gmq gmk cos sin qT k vT gT")


class _ProjOps:
    def __init__(self, u, r, tm, meta):
        self.u, self.r, self.tm, self.meta = u, r, tm, meta
        n_gate = r.wwideT.shape[0] - WIDE_GATE
        self.bounds = [WIDE_GATE + n_gate * e // 8 for e in GATE_SPLIT_EIGHTHS]

    def _wide_rows(self, r0, r1):
        return lax.dot_general(self.r.wwideT[r0:r1, :], self.u, _NT, preferred_element_type=F32)

    def low(self):
        return lax.dot_general(self.r.wlowT[...], self.u, _NT, preferred_element_type=F32)

    def fox(self):
        return self._wide_rows(0, WIDE_GATE)

    def gate_logits(self, j):
        return self._wide_rows(self.bounds[j], self.bounds[j + 1])

    def store_gates(self, j, logits):
        rows = slice(self.bounds[j] - WIDE_GATE, self.bounds[j + 1] - WIDE_GATE)
        self.r.gT[0, rows, :] = jax.nn.sigmoid(logits + self.r.bg[rows, :]).astype(BF16)

    def forget_cumsum(self, low, carry):
        tm = self.tm
        fl = low[LOW_FL:LOW_FL + FOX_HEADS] + self.r.bf[...]
        lf = jnp.minimum(fl, 0.0) - jnp.log1p(jnp.exp(-jnp.abs(fl)))
        r_i = lax.broadcasted_iota(jnp.int32, (tm, tm), 0)
        c_i = lax.broadcasted_iota(jnp.int32, (tm, tm), 1)
        upper = jnp.where(r_i <= c_i, 1.0, 0.0).astype(BF16)
        cs = None
        for part in _bf16_parts(lf):
            d = jnp.dot(part.astype(BF16), upper, preferred_element_type=F32)
            cs = d if cs is None else cs + d
        if self.meta:
            return (cs - cs[:, N_META - 1:N_META]) * LOG2E, None
        return (cs + carry) * LOG2E, carry + cs[:, tm - 1:tm]

    def mla_up(self, low):
        r = self.r
        cq = low[LOW_CQ:LOW_CQ + MLA_Q_RANK]
        cqn = (cq * _rms_rows(cq, MLA_Q_RANK) * r.gcq[...]).astype(BF16)
        qm = jnp.dot(r.wuqT[...], cqn, preferred_element_type=F32)
        ckv = low[LOW_CKV:LOW_CKV + MLA_KV_RANK]
        ckvn = (ckv * _rms_rows(ckv, MLA_KV_RANK) * r.gckv[...]).astype(BF16)
        kv = jnp.dot(r.wukvT[...], ckvn, preferred_element_type=F32)
        return qm, kv

    def store_mla_heads(self, low, qm, kv):
        r, tm = self.r, self.tm
        kr = low[LOW_KR:LOW_KR + MLA_ROPE]
        ss_kr = jnp.sum(kr * kr, axis=0, keepdims=True)
        cos = r.cos[...]
        sin = r.sin[...]
        gmq = r.gmq[...]
        gmk = r.gmk[...]
        zpad = jnp.zeros((HEAD_PAD - MLA_QK, tm), F32)
        for hh in range(MLA_HEADS):
            qh = qm[hh * MLA_QK:(hh + 1) * MLA_QK]
            qn = qh * (_rms_rows(qh, MLA_QK) * (MLA_QK ** -0.5 * LOG2E)) * gmq
            q1, q2 = _rope_rows(qn[MLA_NOPE:], cos, sin)
            q_ext = jnp.concatenate([qn[:MLA_NOPE], q1, q2, zpad], axis=0)
            kn_raw = kv[hh * (MLA_NOPE + MLA_V):hh * (MLA_NOPE + MLA_V) + MLA_NOPE]
            rk = lax.rsqrt((jnp.sum(kn_raw * kn_raw, axis=0, keepdims=True) + ss_kr) * (1.0 / MLA_QK) + EPS)
            kn = kn_raw * rk * gmk[:MLA_NOPE]
            k1, k2 = _rope_rows(kr * rk * gmk[MLA_NOPE:], cos, sin)
            k_ext = jnp.concatenate([kn, k1, k2, zpad], axis=0)
            g = FOX_HEADS + hh
            r.qT[0, g * HEAD_PAD:(g + 1) * HEAD_PAD, :] = q_ext.astype(BF16)
            r.k[0, :, g * HEAD_PAD:(g + 1) * HEAD_PAD] = k_ext.T.astype(BF16)
            r.vT[0, g * V_DIM:(g + 1) * V_DIM, :] = (
                kv[hh * (MLA_NOPE + MLA_V) + MLA_NOPE:(hh + 1) * (MLA_NOPE + MLA_V)].astype(BF16))

    def store_fox_heads(self, fox, c):
        r, tm = self.r, self.tm
        ones = jnp.ones((1, tm), F32)
        one_rows = _rows8([ones, ones, ones], tm)
        zpad = jnp.zeros((HEAD_PAD - FOX_DIM - 16, tm), F32)
        gq = r.gq[...]
        gk = r.gk[...]
        for hh in range(FOX_HEADS):
            qh = fox[WIDE_FQ + hh * FOX_DIM:WIDE_FQ + (hh + 1) * FOX_DIM]
            kh = fox[WIDE_FK + hh * FOX_DIM:WIDE_FK + (hh + 1) * FOX_DIM]
            qn = qh * (_rms_rows(qh, FOX_DIM) * (FOX_DIM ** -0.5 * LOG2E)) * gq
            kn = kh * _rms_rows(kh, FOX_DIM) * gk
            c_rows = _rows8(list(_bf16_parts(c[hh:hh + 1])), tm)
            q_ext = jnp.concatenate([qn, c_rows, one_rows, zpad], axis=0)
            k_ext = jnp.concatenate([kn, one_rows, -c_rows, zpad], axis=0)
            r.qT[0, hh * HEAD_PAD:(hh + 1) * HEAD_PAD, :] = q_ext.astype(BF16)
            r.k[0, :, hh * HEAD_PAD:(hh + 1) * HEAD_PAD] = k_ext.T.astype(BF16)
            r.vT[0, hh * V_DIM:(hh + 1) * V_DIM, :] = (
                fox[WIDE_FV + hh * FOX_DIM:WIDE_FV + (hh + 1) * FOX_DIM].astype(BF16))


def _rms_norm_bf16(x, gain):
    ms = jnp.mean(x * x, axis=-1, keepdims=True)
    return (x * lax.rsqrt(ms + EPS) * gain).astype(BF16)


def _proj_meta_kernel(h_ref, *refs):
    r = _ProjRefs(*refs)
    tm = h_ref.shape[1]
    ops = _ProjOps(_rms_norm_bf16(h_ref[0], r.gain[...]), r, tm, meta=True)
    low = ops.low()
    fox = ops.fox()
    c, _ = ops.forget_cumsum(low, None)
    qm, kv = ops.mla_up(low)
    ops.store_fox_heads(fox, c)
    ops.store_mla_heads(low, qm, kv)
    for j in range(len(ops.bounds) - 1):
        ops.store_gates(j, ops.gate_logits(j))


def _ffn_proj_kernel(x_ref, fgain_ref, wgu_ref, wd_ref, *refs, tiles_per_row):
    *proj_refs, h1_ref, qT_ref, k_ref, vT_ref, gT_ref, hprev_ref, carry_ref = refs
    r = _ProjRefs(*proj_refs, qT_ref, k_ref, vT_ref, gT_ref)
    t = pl.program_id(0)
    tm = x_ref.shape[0]

    @pl.when(t == 0)
    def _():
        hprev_ref[...] = jnp.zeros_like(hprev_ref)
        carry_ref[...] = jnp.zeros_like(carry_ref)

    ops = _ProjOps(_rms_norm_bf16(hprev_ref[...], r.gain[...]), r, tm, meta=False)
    first_of_row = lax.rem(t + (tiles_per_row - 1), tiles_per_row) == 0
    carry = jnp.where(first_of_row, 0.0, carry_ref[:, 0:1])
    x = x_ref[...]

    low = ops.low()
    g, up = _gate_up(x, fgain_ref[...], wgu_ref)
    c, carry = ops.forget_cumsum(low, carry)
    qm, kv = ops.mla_up(low)
    fox = ops.fox()
    ops.store_mla_heads(low, qm, kv)
    h1 = _half_ffn_out(x, g, up, wd_ref)
    ops.store_fox_heads(fox, c)
    n_chunks = len(ops.bounds) - 1
    gate = ops.gate_logits(0)
    for j in range(n_chunks):
        gate_next = ops.gate_logits(j + 1) if j + 1 < n_chunks else None
        ops.store_gates(j, gate)
        gate = gate_next
    h1_ref[...] = h1
    hprev_ref[...] = h1
    carry_ref[...] = jnp.broadcast_to(carry, carry_ref.shape)


def _proj_consts(p):
    return [p["mix_norm"], p["w_lowT"], p["w_wideT"], p["b_forget"], p["b_gate"], p["fox_q_norm"],
            p["fox_k_norm"], p["mla_cq_norm"], p["mla_w_uqT"], p["mla_ckv_norm"], p["mla_w_ukvT"],
            p["mla_q_norm"], p["mla_k_norm"]]


def _proj_out_shapes(b, s, n_gate):
    return (
        jax.ShapeDtypeStruct((b, N_HEADS * HEAD_PAD, s), BF16),
        jax.ShapeDtypeStruct((b, s, N_HEADS * HEAD_PAD), BF16),
        jax.ShapeDtypeStruct((b, N_HEADS * V_DIM, s), BF16),
        jax.ShapeDtypeStruct((b, n_gate, s), BF16),
    )


def _proj_meta(h3d, cosT, sinT, p):
    _, tm, d = h3d.shape
    n_gate = p["w_wideT"].shape[0] - WIDE_GATE
    consts = _proj_consts(p)
    whole = lambda shape: pl.BlockSpec(shape, lambda i: (0,) * len(shape))
    return pl.pallas_call(
        _proj_meta_kernel,
        out_shape=_proj_out_shapes(1, tm, n_gate),
        grid=(1,),
        in_specs=[whole(h3d.shape)] + [whole(c.shape) for c in consts] + [whole(cosT.shape), whole(sinT.shape)],
        out_specs=tuple(whole(o.shape) for o in _proj_out_shapes(1, tm, n_gate)),
        compiler_params=pltpu.CompilerParams(
            dimension_semantics=("arbitrary",), vmem_limit_bytes=V7X_VMEM_LIMIT_BYTES),
        name="proj_meta",
    )(h3d, *consts, cosT, sinT)


def _ffn_proj(x2d, fgain, w_gu, w_down, cosT, sinT, p, *, b, s, tm):
    n, d = x2d.shape
    n_gate = p["w_wideT"].shape[0] - WIDE_GATE
    consts = _proj_consts(p)
    tiles_per_row = s // tm
    n_tiles = n // tm

    def cur(t):
        return jnp.minimum(t, n_tiles - 1)

    def prev_row(t):
        tt = jnp.maximum(t - 1, 0)
        return tt // tiles_per_row, tt % tiles_per_row

    outs = pl.pallas_call(
        functools.partial(_ffn_proj_kernel, tiles_per_row=tiles_per_row),
        out_shape=(jax.ShapeDtypeStruct((n, d), F32),) + _proj_out_shapes(b, s, n_gate),
        grid=(n_tiles + 1,),
        in_specs=[
            pl.BlockSpec((tm, d), lambda t: (cur(t), 0)),
            _const_spec(fgain.shape),
            _const_spec(w_gu.shape),
            _const_spec(w_down.shape),
        ] + [_const_spec(c.shape) for c in consts]
        + [pl.BlockSpec((MLA_ROPE // 2, tm), lambda t: (0, prev_row(t)[1]))] * 2,
        out_specs=(
            pl.BlockSpec((tm, d), lambda t: (cur(t), 0)),
            pl.BlockSpec((1, N_HEADS * HEAD_PAD, tm), lambda t: (prev_row(t)[0], 0, prev_row(t)[1])),
            pl.BlockSpec((1, tm, N_HEADS * HEAD_PAD), lambda t: (prev_row(t)[0], prev_row(t)[1], 0)),
            pl.BlockSpec((1, N_HEADS * V_DIM, tm), lambda t: (prev_row(t)[0], 0, prev_row(t)[1])),
            pl.BlockSpec((1, n_gate, tm), lambda t: (prev_row(t)[0], 0, prev_row(t)[1])),
        ),
        scratch_shapes=[pltpu.VMEM((tm, d), F32), pltpu.VMEM((FOX_HEADS, 128), F32)],
        compiler_params=pltpu.CompilerParams(
            dimension_semantics=("arbitrary",), vmem_limit_bytes=V7X_VMEM_LIMIT_BYTES),
        name="ffn_proj",
    )(x2d, fgain, w_gu, w_down, *consts, cosT, sinT)
    return outs


def _attn_kernel(qT_ref, k_ref, vT_ref, km_ref, vmT_ref, o_ref, *, tq, group):
    s_len = qT_ref.shape[2]
    row = lax.broadcasted_iota(jnp.int32, (tq, tq), 0)
    col = lax.broadcasted_iota(jnp.int32, (tq, tq), 1)
    causal = row <= col
    ones_real = jnp.ones((16, s_len), BF16)
    ones_meta = jnp.ones((16, N_META), BF16)
    v_ext = [jnp.concatenate([vT_ref[0, g * V_DIM:(g + 1) * V_DIM, :], ones_real], axis=0)
             for g in range(group)]
    vm_ext = [jnp.concatenate([vmT_ref[g * V_DIM:(g + 1) * V_DIM, :], ones_meta], axis=0)
              for g in range(group)]

    def scores(i, g):
        qT = qT_ref[0, g * HEAD_PAD:(g + 1) * HEAD_PAD, i * tq:(i + 1) * tq]
        n = (i + 1) * tq
        s = jnp.dot(k_ref[0, :n, g * HEAD_PAD:(g + 1) * HEAD_PAD], qT, preferred_element_type=F32)
        sm = jnp.dot(km_ref[:, g * HEAD_PAD:(g + 1) * HEAD_PAD], qT, preferred_element_type=F32)
        return s, sm

    def finish(i, g, s, sm):
        n = (i + 1) * tq
        diag = jnp.where(causal, s[n - tq:], MASK_VALUE)
        m = jnp.maximum(jnp.max(diag, axis=0, keepdims=True), jnp.max(sm, axis=0, keepdims=True))
        parts = [jnp.exp2(diag - m).astype(BF16)]
        if i > 0:
            m = jnp.maximum(m, jnp.max(s[:n - tq], axis=0, keepdims=True))
            parts = [jnp.exp2(s[:n - tq] - m).astype(BF16), jnp.exp2(diag - m).astype(BF16)]
        p = jnp.concatenate(parts, axis=0) if len(parts) > 1 else parts[0]
        pm = jnp.exp2(sm - m).astype(BF16)
        o = (jnp.dot(v_ext[g][:, :n], p, preferred_element_type=F32)
             + jnp.dot(vm_ext[g], pm, preferred_element_type=F32))
        out = o[:V_DIM] * (1.0 / o[V_DIM:V_DIM + 1])
        o_ref[0, g * V_DIM:(g + 1) * V_DIM, i * tq:(i + 1) * tq] = out.astype(BF16)

    nq = s_len // tq
    order = [t for pair in zip(range(nq - 1, -1, -1), range(nq)) for t in pair][:nq]
    pending = []
    for i in order:
        for g in range(group):
            pending.append((i, g) + scores(i, g))
            if len(pending) > ATTN_LOOKAHEAD:
                finish(*pending.pop(0))
    for unit in pending:
        finish(*unit)


def _attention(qT, k, vT, k_meta, vT_meta, *, tq, group):
    b, _, s = qT.shape
    return pl.pallas_call(
        functools.partial(_attn_kernel, tq=tq, group=group),
        out_shape=jax.ShapeDtypeStruct((b, N_HEADS * V_DIM, s), BF16),
        grid=(b, N_HEADS // group),
        in_specs=[
            pl.BlockSpec((1, group * HEAD_PAD, s), lambda i, h: (i, h, 0)),
            pl.BlockSpec((1, s, group * HEAD_PAD), lambda i, h: (i, 0, h)),
            pl.BlockSpec((1, group * V_DIM, s), lambda i, h: (i, h, 0)),
            pl.BlockSpec((N_META, group * HEAD_PAD), lambda i, h: (0, h)),
            pl.BlockSpec((group * V_DIM, N_META), lambda i, h: (h, 0)),
        ],
        out_specs=pl.BlockSpec((1, group * V_DIM, s), lambda i, h: (i, h, 0)),
        compiler_params=pltpu.CompilerParams(
            dimension_semantics=("arbitrary", "arbitrary"), vmem_limit_bytes=V7X_VMEM_LIMIT_BYTES),
        name="attention",
    )(qT, k, vT, k_meta, vT_meta)


def _out_ffn_kernel(oT_ref, gT_ref, h_ref, wbfT_ref, wbmT_ref, woT_ref, gain_ref, wgu_ref, wd_ref, o_ref, *, sub):
    tm = h_ref.shape[1]
    d = woT_ref.shape[0]
    n_sub = tm // sub

    def branches(j):
        lanes = slice(j * sub, (j + 1) * sub)
        yf = jnp.dot(wbfT_ref[...], oT_ref[0, :FOX_W, lanes], preferred_element_type=F32)
        ym = jnp.dot(wbmT_ref[...], oT_ref[0, FOX_W:, lanes], preferred_element_type=F32)
        gf = gT_ref[0, :d, lanes].astype(F32)
        gm = gT_ref[0, d:, lanes].astype(F32)
        return (gf * yf + gm * ym).astype(BF16)

    def mixed(j, z):
        mixT = jnp.dot(woT_ref[...], z, preferred_element_type=F32)
        return h_ref[0, j * sub:(j + 1) * sub, :] + mixT.T

    zs = [branches(j) for j in range(n_sub)]
    h2 = [mixed(j, zs[j]) for j in range(n_sub)]
    gu = [_gate_up(h2[j], gain_ref[...], wgu_ref) for j in range(n_sub)]
    for j in range(n_sub):
        o_ref[0, j * sub:(j + 1) * sub, :] = _half_ffn_out(h2[j], *gu[j], wd_ref)


def _out_ffn(oT, gT, h3d, wbfT, wbmT, woT, gain, w_gu, w_down, *, tm, sub):
    b, s, d = h3d.shape
    return pl.pallas_call(
        functools.partial(_out_ffn_kernel, sub=sub),
        out_shape=jax.ShapeDtypeStruct((b, s, d), F32),
        grid=(b, s // tm),
        in_specs=[
            pl.BlockSpec((1, oT.shape[1], tm), lambda i, t: (i, 0, t)),
            pl.BlockSpec((1, gT.shape[1], tm), lambda i, t: (i, 0, t)),
            pl.BlockSpec((1, tm, d), lambda i, t: (i, t, 0)),
            _const_spec(wbfT.shape),
            _const_spec(wbmT.shape),
            _const_spec(woT.shape),
            _const_spec(gain.shape),
            _const_spec(w_gu.shape),
            _const_spec(w_down.shape),
        ],
        out_specs=pl.BlockSpec((1, tm, d), lambda i, t: (i, t, 0)),
        compiler_params=pltpu.CompilerParams(
            dimension_semantics=("arbitrary", "arbitrary"), vmem_limit_bytes=V7X_VMEM_LIMIT_BYTES),
        name="out_ffn",
    )(oT, gT, h3d, wbfT, wbmT, woT, gain, w_gu, w_down)


def _col(v):
    return v.astype(F32).reshape(-1, 1)


def _pick_tile(n, pref):
    t = min(n, pref)
    while n % t:
        t //= 2
    return t


def kernel(x, meta_tokens, ffn1_norm, ffn1_w_gu, ffn1_w_down, mix_norm, w_in, b_forget, b_gate, fox_q_norm,
           fox_k_norm, mla_cq_norm, mla_w_uq, mla_ckv_norm, mla_w_ukv, mla_q_norm, mla_k_norm, w_branch_fox,
           w_branch_mla, w_out, ffn2_norm, ffn2_w_gu, ffn2_w_down):
    b, s, d = x.shape
    depth = ffn1_norm.shape[0]
    if depth != 1:
        raise NotImplementedError("only depth 1 is supported")
    meta_rows = 128

    pos = jnp.arange(N_META + s, dtype=F32)
    inv_freq = ROPE_THETA ** (-jnp.arange(0, MLA_ROPE, 2, dtype=F32) / MLA_ROPE)
    ang = pos[:, None] * inv_freq[None, :]
    cosT, sinT = jnp.cos(ang).T, jnp.sin(ang).T
    pad = ((0, 0), (0, meta_rows - N_META))
    cos_meta, sin_meta = jnp.pad(cosT[:, :N_META], pad), jnp.pad(sinT[:, :N_META], pad)
    cos_real, sin_real = cosT[:, N_META:], sinT[:, N_META:]

    tm_fused = _pick_tile(s, 256)
    tm_proj = _pick_tile(s, 512)
    sub_proj = _pick_tile(tm_proj, 256)
    tq = _pick_tile(s, 256)

    w1gu, w1d = ffn1_w_gu[0].astype(BF16), ffn1_w_down[0].astype(BF16)
    w2gu, w2d = ffn2_w_gu[0].astype(BF16), ffn2_w_down[0].astype(BF16)
    g1, g2 = ffn1_norm[0].reshape(1, d).astype(F32), ffn2_norm[0].reshape(1, d).astype(F32)
    p = {
        "mix_norm": mix_norm[0].reshape(1, d).astype(F32),
        "w_lowT": w_in[0][:, OFF_FL:OFF_GATE].T.astype(BF16),
        "w_wideT": jnp.concatenate([w_in[0][:, :OFF_FL], w_in[0][:, OFF_GATE:]], axis=1).T.astype(BF16),
        "b_forget": _col(b_forget[0]),
        "b_gate": _col(b_gate[0]),
        "fox_q_norm": _col(fox_q_norm[0]),
        "fox_k_norm": _col(fox_k_norm[0]),
        "mla_cq_norm": _col(mla_cq_norm[0]),
        "mla_w_uqT": mla_w_uq[0].T.astype(BF16),
        "mla_ckv_norm": _col(mla_ckv_norm[0]),
        "mla_w_ukvT": mla_w_ukv[0].T.astype(BF16),
        "mla_q_norm": _col(mla_q_norm[0]),
        "mla_k_norm": _col(mla_k_norm[0]),
    }
    wbfT = w_branch_fox[0].T.astype(BF16)
    wbmT = w_branch_mla[0].T.astype(BF16)
    woT = w_out[0].T.astype(BF16)

    hm = jnp.pad(meta_tokens.astype(F32), ((0, meta_rows - N_META), (0, 0)))
    hm1 = _ffn(hm, g1, w1gu, w1d, tm=meta_rows)
    _, km, vmT, _ = _proj_meta(hm1[None], cos_meta, sin_meta, p)
    k_meta, vT_meta = km[0, :N_META], vmT[0, :, :N_META]

    h1, qT, k, vT, gT = _ffn_proj(x.astype(F32).reshape(b * s, d), g1, w1gu, w1d, cos_real, sin_real, p,
                                  b=b, s=s, tm=tm_fused)
    h1 = h1.reshape(b, s, d)
    oT = _attention(qT, k, vT, k_meta, vT_meta, tq=tq, group=ATTN_GROUP)
    h3 = _out_ffn(oT, gT, h1, wbfT, wbmT, woT, g2, w2gu, w2d, tm=tm_proj, sub=sub_proj)
    return h3.astype(x.dtype)
```

```python
import collections
import functools

import jax
import jax.numpy as jnp
from jax import lax
from jax.experimental import pallas as pl
from jax.experimental.pallas import tpu as pltpu

F32 = jnp.float32
BF16 = jnp.bfloat16

EPS = 1e-6
N_META = 16
FOX_HEADS = 8
FOX_DIM = 64
FOX_W = FOX_HEADS * FOX_DIM
MLA_HEADS = 8
MLA_Q_RANK = 256
MLA_KV_RANK = 128
MLA_NOPE = 64
MLA_ROPE = 32
MLA_QK = MLA_NOPE + MLA_ROPE
MLA_V = 64
ROPE_THETA = 10000.0
N_BRANCH = 2
LOG2E = 1.4426950408889634

N_HEADS = FOX_HEADS + MLA_HEADS
HEAD_PAD = 128
V_DIM = 64

OFF_FQ = 0
OFF_FK = OFF_FQ + FOX_W
OFF_FV = OFF_FK + FOX_W
OFF_FL = OFF_FV + FOX_W
OFF_CQ = OFF_FL + FOX_HEADS
OFF_CKV = OFF_CQ + MLA_Q_RANK
OFF_KR = OFF_CKV + MLA_KV_RANK
OFF_GATE = OFF_KR + MLA_ROPE
LOW_FL = 0
LOW_CQ = LOW_FL + FOX_HEADS
LOW_CKV = LOW_CQ + MLA_Q_RANK
LOW_KR = LOW_CKV + MLA_KV_RANK
LOW_ROWS = LOW_KR + MLA_ROPE
WIDE_FQ = 0
WIDE_FK = WIDE_FQ + FOX_W
WIDE_FV = WIDE_FK + FOX_W
WIDE_GATE = WIDE_FV + FOX_W

V7X_VMEM_LIMIT_BYTES = 56 * 1024 * 1024
MASK_VALUE = -1e30
ATTN_GROUP = 8
ATTN_LOOKAHEAD = 3
GATE_SPLIT_EIGHTHS = (0, 2, 4, 7, 8)

_NT = (((1,), (1,)), ((), ()))
_TN = (((0,), (0,)), ((), ()))


def _const_spec(shape):
    zeros = (0,) * len(shape)
    return pl.BlockSpec(shape, lambda *_: zeros, pipeline_mode=pl.Buffered(1))


def _gate_up(x, gain, wgu_ref):
    d_ff = wgu_ref.shape[1] // 2
    ms = jnp.mean(x * x, axis=-1, keepdims=True)
    u = (x * lax.rsqrt(ms + EPS) * gain).astype(BF16)
    g = jnp.dot(u, wgu_ref[:, :d_ff], preferred_element_type=F32)
    up = jnp.dot(u, wgu_ref[:, d_ff:], preferred_element_type=F32)
    return g, up


def _half_ffn_out(x, g, up, wd_ref):
    a = (g * jax.nn.sigmoid(g) * up).astype(BF16)
    return x + 0.5 * jnp.dot(a, wd_ref[...], preferred_element_type=F32)


def _ffn_kernel(x_ref, gain_ref, wgu_ref, wd_ref, o_ref):
    x = x_ref[...]
    g, up = _gate_up(x, gain_ref[...], wgu_ref)
    o_ref[...] = _half_ffn_out(x, g, up, wd_ref)


def _ffn(x2d, gain, w_gu, w_down, *, tm):
    n, d = x2d.shape
    d_ff = w_down.shape[0]
    return pl.pallas_call(
        _ffn_kernel,
        out_shape=jax.ShapeDtypeStruct((n, d), F32),
        grid=(n // tm,),
        in_specs=[
            pl.BlockSpec((tm, d), lambda i: (i, 0)),
            _const_spec((1, d)),
            _const_spec((d, 2 * d_ff)),
            _const_spec((d_ff, d)),
        ],
        out_specs=pl.BlockSpec((tm, d), lambda i: (i, 0)),
        compiler_params=pltpu.CompilerParams(
            dimension_semantics=("arbitrary",), vmem_limit_bytes=V7X_VMEM_LIMIT_BYTES),
        name="ffn",
    )(x2d, gain, w_gu, w_down)


def _bf16_parts(c):
    hi = c.astype(BF16).astype(F32)
    r = c - hi
    mid = r.astype(BF16).astype(F32)
    lo = (r - mid).astype(BF16).astype(F32)
    return hi, mid, lo


def _rows8(vals, tm):
    row = lax.broadcasted_iota(jnp.int32, (8, tm), 0)
    out = jnp.zeros((8, tm), F32)
    for j, v in enumerate(vals):
        out = jnp.where(row == j, v, out)
    return out


def _rms_rows(x, n):
    return lax.rsqrt(jnp.sum(x * x, axis=0, keepdims=True) * (1.0 / n) + EPS)


def _rope_rows(x, cos, sin):
    x1 = x[:MLA_ROPE // 2]
    x2 = x[MLA_ROPE // 2:]
    return x1 * cos - x2 * sin, x1 * sin + x2 * cos


_ProjRefs = collections.namedtuple(
    "_ProjRefs", "gain wlowT wwideT bf bg gq gk gcq wuqT gckv wukvT gmq gmk cos sin qT k vT gT")


class _ProjOps:
    def __init__(self, u, r, tm, meta):
        self.u, self.r, self.tm, self.meta = u, r, tm, meta
        n_gate = r.wwideT.shape[0] - WIDE_GATE
        self.bounds = [WIDE_GATE + n_gate * e // 8 for e in GATE_SPLIT_EIGHTHS]

    def _wide_rows(self, r0, r1):
        return lax.dot_general(self.r.wwideT[r0:r1, :], self.u, _NT, preferred_element_type=F32)

    def low(self):
        return lax.dot_general(self.r.wlowT[...], self.u, _NT, preferred_element_type=F32)

    def fox(self):
        return self._wide_rows(0, WIDE_GATE)

    def gate_logits(self, j):
        return self._wide_rows(self.bounds[j], self.bounds[j + 1])

    def store_gates(self, j, logits):
        rows = slice(self.bounds[j] - WIDE_GATE, self.bounds[j + 1] - WIDE_GATE)
        self.r.gT[0, rows, :] = jax.nn.sigmoid(logits + self.r.bg[rows, :]).astype(BF16)

    def forget_cumsum(self, low, carry):
        tm = self.tm
        fl = low[LOW_FL:LOW_FL + FOX_HEADS] + self.r.bf[...]
        lf = jnp.minimum(fl, 0.0) - jnp.log1p(jnp.exp(-jnp.abs(fl)))
        r_i = lax.broadcasted_iota(jnp.int32, (tm, tm), 0)
        c_i = lax.broadcasted_iota(jnp.int32, (tm, tm), 1)
        upper = jnp.where(r_i <= c_i, 1.0, 0.0).astype(BF16)
        cs = None
        for part in _bf16_parts(lf):
            d = jnp.dot(part.astype(BF16), upper, preferred_element_type=F32)
            cs = d if cs is None else cs + d
        if self.meta:
            return (cs - cs[:, N_META - 1:N_META]) * LOG2E, None
        return (cs + carry) * LOG2E, carry + cs[:, tm - 1:tm]

    def mla_up(self, low):
        r = self.r
        cq = low[LOW_CQ:LOW_CQ + MLA_Q_RANK]
        cqn = (cq * _rms_rows(cq, MLA_Q_RANK) * r.gcq[...]).astype(BF16)
        qm = jnp.dot(r.wuqT[...], cqn, preferred_element_type=F32)
        ckv = low[LOW_CKV:LOW_CKV + MLA_KV_RANK]
        ckvn = (ckv * _rms_rows(ckv, MLA_KV_RANK) * r.gckv[...]).astype(BF16)
        kv = jnp.dot(r.wukvT[...], ckvn, preferred_element_type=F32)
        return qm, kv

    def store_mla_heads(self, low, qm, kv):
        r, tm = self.r, self.tm
        kr = low[LOW_KR:LOW_KR + MLA_ROPE]
        ss_kr = jnp.sum(kr * kr, axis=0, keepdims=True)
        cos = r.cos[...]
        sin = r.sin[...]
        gmq = r.gmq[...]
        gmk = r.gmk[...]
        zpad = jnp.zeros((HEAD_PAD - MLA_QK, tm), F32)
        for hh in range(MLA_HEADS):
            qh = qm[hh * MLA_QK:(hh + 1) * MLA_QK]
            qn = qh * (_rms_rows(qh, MLA_QK) * (MLA_QK ** -0.5 * LOG2E)) * gmq
            q1, q2 = _rope_rows(qn[MLA_NOPE:], cos, sin)
            q_ext = jnp.concatenate([qn[:MLA_NOPE], q1, q2, zpad], axis=0)
            kn_raw = kv[hh * (MLA_NOPE + MLA_V):hh * (MLA_NOPE + MLA_V) + MLA_NOPE]
            rk = lax.rsqrt((jnp.sum(kn_raw * kn_raw, axis=0, keepdims=True) + ss_kr) * (1.0 / MLA_QK) + EPS)
            kn = kn_raw * rk * gmk[:MLA_NOPE]
            k1, k2 = _rope_rows(kr * rk * gmk[MLA_NOPE:], cos, sin)
            k_ext = jnp.concatenate([kn, k1, k2, zpad], axis=0)
            g = FOX_HEADS + hh
            r.qT[0, g * HEAD_PAD:(g + 1) * HEAD_PAD, :] = q_ext.astype(BF16)
            r.k[0, :, g * HEAD_PAD:(g + 1) * HEAD_PAD] = k_ext.T.astype(BF16)
            r.vT[0, g * V_DIM:(g + 1) * V_DIM, :] = (
                kv[hh * (MLA_NOPE + MLA_V) + MLA_NOPE:(hh + 1) * (MLA_NOPE + MLA_V)].astype(BF16))

    def store_fox_heads(self, fox, c):
        r, tm = self.r, self.tm
        ones = jnp.ones((1, tm), F32)
        one_rows = _rows8([ones, ones, ones], tm)
        zpad = jnp.zeros((HEAD_PAD - FOX_DIM - 16, tm), F32)
        gq = r.gq[...]
        gk = r.gk[...]
        for hh in range(FOX_HEADS):
            qh = fox[WIDE_FQ + hh * FOX_DIM:WIDE_FQ + (hh + 1) * FOX_DIM]
            kh = fox[WIDE_FK + hh * FOX_DIM:WIDE_FK + (hh + 1) * FOX_DIM]
            qn = qh * (_rms_rows(qh, FOX_DIM) * (FOX_DIM ** -0.5 * LOG2E)) * gq
            kn = kh * _rms_rows(kh, FOX_DIM) * gk
            c_rows = _rows8(list(_bf16_parts(c[hh:hh + 1])), tm)
            q_ext = jnp.concatenate([qn, c_rows, one_rows, zpad], axis=0)
            k_ext = jnp.concatenate([kn, one_rows, -c_rows, zpad], axis=0)
            r.qT[0, hh * HEAD_PAD:(hh + 1) * HEAD_PAD, :] = q_ext.astype(BF16)
            r.k[0, :, hh * HEAD_PAD:(hh + 1) * HEAD_PAD] = k_ext.T.astype(BF16)
            r.vT[0, hh * V_DIM:(hh + 1) * V_DIM, :] = (
                fox[WIDE_FV + hh * FOX_DIM:WIDE_FV + (hh + 1) * FOX_DIM].astype(BF16))


def _rms_norm_bf16(x, gain):
    ms = jnp.mean(x * x, axis=-1, keepdims=True)
    return (x * lax.rsqrt(ms + EPS) * gain).astype(BF16)


def _proj_meta_kernel(h_ref, *refs):
    r = _ProjRefs(*refs)
    tm = h_ref.shape[1]
    ops = _ProjOps(_rms_norm_bf16(h_ref[0], r.gain[...]), r, tm, meta=True)
    low = ops.low()
    fox = ops.fox()
    c, _ = ops.forget_cumsum(low, None)
    qm, kv = ops.mla_up(low)
    ops.store_fox_heads(fox, c)
    ops.store_mla_heads(low, qm, kv)
    for j in range(len(ops.bounds) - 1):
        ops.store_gates(j, ops.gate_logits(j))


def _ffn_proj_kernel(x_ref, fgain_ref, wgu_ref, wd_ref, *refs, tiles_per_row):
    *proj_refs, h1_ref, qT_ref, k_ref, vT_ref, gT_ref, uprev_ref, carry_ref = refs
    r = _ProjRefs(*proj_refs, qT_ref, k_ref, vT_ref, gT_ref)
    t = pl.program_id(0)
    tm = x_ref.shape[0]

    @pl.when(t == 0)
    def _():
        uprev_ref[...] = jnp.zeros_like(uprev_ref)
        carry_ref[...] = jnp.zeros_like(carry_ref)

    ops = _ProjOps(uprev_ref[...], r, tm, meta=False)
    first_of_row = lax.rem(t + (tiles_per_row - 1), tiles_per_row) == 0
    carry = jnp.where(first_of_row, 0.0, carry_ref[:, 0:1])
    x = x_ref[...]

    low = ops.low()
    g, up = _gate_up(x, fgain_ref[...], wgu_ref)
    c, carry = ops.forget_cumsum(low, carry)
    qm, kv = ops.mla_up(low)
    fox = ops.fox()
    ops.store_mla_heads(low, qm, kv)
    h1 = _half_ffn_out(x, g, up, wd_ref)
    ops.store_fox_heads(fox, c)
    n_chunks = len(ops.bounds) - 1
    gate = ops.gate_logits(0)
    for j in range(n_chunks):
        gate_next = ops.gate_logits(j + 1) if j + 1 < n_chunks else None
        ops.store_gates(j, gate)
        gate = gate_next
    h1_ref[...] = h1
    uprev_ref[...] = _rms_norm_bf16(h1, r.gain[...])
    carry_ref[...] = jnp.broadcast_to(carry, carry_ref.shape)


def _proj_consts(p):
    return [p["mix_norm"], p["w_lowT"], p["w_wideT"], p["b_forget"], p["b_gate"], p["fox_q_norm"],
            p["fox_k_norm"], p["mla_cq_norm"], p["mla_w_uqT"], p["mla_ckv_norm"], p["mla_w_ukvT"],
            p["mla_q_norm"], p["mla_k_norm"]]


def _proj_out_shapes(b, s, n_gate):
    return (
        jax.ShapeDtypeStruct((b, N_HEADS * HEAD_PAD, s), BF16),
        jax.ShapeDtypeStruct((b, s, N_HEADS * HEAD_PAD), BF16),
        jax.ShapeDtypeStruct((b, N_HEADS * V_DIM, s), BF16),
        jax.ShapeDtypeStruct((b, n_gate, s), BF16),
    )


def _proj_meta(h3d, cosT, sinT, p):
    _, tm, d = h3d.shape
    n_gate = p["w_wideT"].shape[0] - WIDE_GATE
    consts = _proj_consts(p)
    whole = lambda shape: pl.BlockSpec(shape, lambda i: (0,) * len(shape))
    return pl.pallas_call(
        _proj_meta_kernel,
        out_shape=_proj_out_shapes(1, tm, n_gate),
        grid=(1,),
        in_specs=[whole(h3d.shape)] + [whole(c.shape) for c in consts] + [whole(cosT.shape), whole(sinT.shape)],
        out_specs=tuple(whole(o.shape) for o in _proj_out_shapes(1, tm, n_gate)),
        compiler_params=pltpu.CompilerParams(
            dimension_semantics=("arbitrary",), vmem_limit_bytes=V7X_VMEM_LIMIT_BYTES),
        name="proj_meta",
    )(h3d, *consts, cosT, sinT)


def _ffn_proj(x2d, fgain, w_gu, w_down, cosT, sinT, p, *, b, s, tm):
    n, d = x2d.shape
    n_gate = p["w_wideT"].shape[0] - WIDE_GATE
    consts = _proj_consts(p)
    tiles_per_row = s // tm
    n_tiles = n // tm

    def cur(t):
        return jnp.minimum(t, n_tiles - 1)

    def prev_row(t):
        tt = jnp.maximum(t - 1, 0)
        return tt // tiles_per_row, tt % tiles_per_row

    outs = pl.pallas_call(
        functools.partial(_ffn_proj_kernel, tiles_per_row=tiles_per_row),
        out_shape=(jax.ShapeDtypeStruct((n, d), F32),) + _proj_out_shapes(b, s, n_gate),
        grid=(n_tiles + 1,),
        in_specs=[
            pl.BlockSpec((tm, d), lambda t: (cur(t), 0)),
            _const_spec(fgain.shape),
            _const_spec(w_gu.shape),
            _const_spec(w_down.shape),
        ] + [_const_spec(c.shape) for c in consts]
        + [pl.BlockSpec((MLA_ROPE // 2, tm), lambda t: (0, prev_row(t)[1]))] * 2,
        out_specs=(
            pl.BlockSpec((tm, d), lambda t: (cur(t), 0)),
            pl.BlockSpec((1, N_HEADS * HEAD_PAD, tm), lambda t: (prev_row(t)[0], 0, prev_row(t)[1])),
            pl.BlockSpec((1, tm, N_HEADS * HEAD_PAD), lambda t: (prev_row(t)[0], prev_row(t)[1], 0)),
            pl.BlockSpec((1, N_HEADS * V_DIM, tm), lambda t: (prev_row(t)[0], 0, prev_row(t)[1])),
            pl.BlockSpec((1, n_gate, tm), lambda t: (prev_row(t)[0], 0, prev_row(t)[1])),
        ),
        scratch_shapes=[pltpu.VMEM((tm, d), BF16), pltpu.VMEM((FOX_HEADS, 128), F32)],
        compiler_params=pltpu.CompilerParams(
            dimension_semantics=("arbitrary",), vmem_limit_bytes=V7X_VMEM_LIMIT_BYTES),
        name="ffn_proj",
    )(x2d, fgain, w_gu, w_down, *consts, cosT, sinT)
    return outs


def _attn_kernel(qT_ref, k_ref, vT_ref, km_ref, vmT_ref, o_ref, *, tq, group):
    s_len = qT_ref.shape[2]
    row = lax.broadcasted_iota(jnp.int32, (tq, tq), 0)
    col = lax.broadcasted_iota(jnp.int32, (tq, tq), 1)
    causal = row <= col
    ones_real = jnp.ones((16, s_len), BF16)
    ones_meta = jnp.ones((16, N_META), BF16)
    v_ext = [jnp.concatenate([vT_ref[0, g * V_DIM:(g + 1) * V_DIM, :], ones_real], axis=0)
             for g in range(group)]
    vm_ext = [jnp.concatenate([vmT_ref[g * V_DIM:(g + 1) * V_DIM, :], ones_meta], axis=0)
              for g in range(group)]

    def scores(i, g):
        qT = qT_ref[0, g * HEAD_PAD:(g + 1) * HEAD_PAD, i * tq:(i + 1) * tq]
        n = (i + 1) * tq
        s = jnp.dot(k_ref[0, :n, g * HEAD_PAD:(g + 1) * HEAD_PAD], qT, preferred_element_type=F32)
        sm = jnp.dot(km_ref[:, g * HEAD_PAD:(g + 1) * HEAD_PAD], qT, preferred_element_type=F32)
        return s, sm

    def finish(i, g, s, sm):
        n = (i + 1) * tq
        diag = jnp.where(causal, s[n - tq:], MASK_VALUE)
        m = jnp.maximum(jnp.max(diag, axis=0, keepdims=True), jnp.max(sm, axis=0, keepdims=True))
        parts = [jnp.exp2(diag - m).astype(BF16)]
        if i > 0:
            m = jnp.maximum(m, jnp.max(s[:n - tq], axis=0, keepdims=True))
            parts = [jnp.exp2(s[:n - tq] - m).astype(BF16), jnp.exp2(diag - m).astype(BF16)]
        p = jnp.concatenate(parts, axis=0) if len(parts) > 1 else parts[0]
        pm = jnp.exp2(sm - m).astype(BF16)
        o = (jnp.dot(v_ext[g][:, :n], p, preferred_element_type=F32)
             + jnp.dot(vm_ext[g], pm, preferred_element_type=F32))
        out = o[:V_DIM] * (1.0 / o[V_DIM:V_DIM + 1])
        o_ref[0, g * V_DIM:(g + 1) * V_DIM, i * tq:(i + 1) * tq] = out.astype(BF16)

    nq = s_len // tq
    order = [t for pair in zip(range(nq - 1, -1, -1), range(nq)) for t in pair][:nq]
    pending = []
    for i in order:
        for g in range(group):
            pending.append((i, g) + scores(i, g))
            if len(pending) > ATTN_LOOKAHEAD:
                finish(*pending.pop(0))
    for unit in pending:
        finish(*unit)


def _attention(qT, k, vT, k_meta, vT_meta, *, tq, group):
    b, _, s = qT.shape
    return pl.pallas_call(
        functools.partial(_attn_kernel, tq=tq, group=group),
        out_shape=jax.ShapeDtypeStruct((b, N_HEADS * V_DIM, s), BF16),
        grid=(b, N_HEADS // group),
        in_specs=[
            pl.BlockSpec((1, group * HEAD_PAD, s), lambda i, h: (i, h, 0)),
            pl.BlockSpec((1, s, group * HEAD_PAD), lambda i, h: (i, 0, h)),
            pl.BlockSpec((1, group * V_DIM, s), lambda i, h: (i, h, 0)),
            pl.BlockSpec((N_META, group * HEAD_PAD), lambda i, h: (0, h)),
            pl.BlockSpec((group * V_DIM, N_META), lambda i, h: (h, 0)),
        ],
        out_specs=pl.BlockSpec((1, group * V_DIM, s), lambda i, h: (i, h, 0)),
        compiler_params=pltpu.CompilerParams(
            dimension_semantics=("arbitrary", "arbitrary"), vmem_limit_bytes=V7X_VMEM_LIMIT_BYTES),
        name="attention",
    )(qT, k, vT, k_meta, vT_meta)


def _out_ffn_kernel(oT_ref, gT_ref, h_ref, wbfT_ref, wbmT_ref, woT_ref, gain_ref, wgu_ref, wd_ref, o_ref, *, sub):
    tm = h_ref.shape[1]
    d = woT_ref.shape[0]
    n_sub = tm // sub

    def branches(j):
        lanes = slice(j * sub, (j + 1) * sub)
        yf = jnp.dot(wbfT_ref[...], oT_ref[0, :FOX_W, lanes], preferred_element_type=F32)
        ym = jnp.dot(wbmT_ref[...], oT_ref[0, FOX_W:, lanes], preferred_element_type=F32)
        gf = gT_ref[0, :d, lanes].astype(F32)
        gm = gT_ref[0, d:, lanes].astype(F32)
        return (gf * yf + gm * ym).astype(BF16)

    def mixed(j, z):
        mix = lax.dot_general(z, woT_ref[...], _TN, preferred_element_type=F32)
        return h_ref[0, j * sub:(j + 1) * sub, :] + mix

    zs = [branches(j) for j in range(n_sub)]
    h2 = [mixed(j, zs[j]) for j in range(n_sub)]
    gu = [_gate_up(h2[j], gain_ref[...], wgu_ref) for j in range(n_sub)]
    for j in range(n_sub):
        o_ref[0, j * sub:(j + 1) * sub, :] = _half_ffn_out(h2[j], *gu[j], wd_ref)


def _out_ffn(oT, gT, h3d, wbfT, wbmT, woT, gain, w_gu, w_down, *, tm, sub):
    b, s, d = h3d.shape
    return pl.pallas_call(
        functools.partial(_out_ffn_kernel, sub=sub),
        out_shape=jax.ShapeDtypeStruct((b, s, d), F32),
        grid=(b, s // tm),
        in_specs=[
            pl.BlockSpec((1, oT.shape[1], tm), lambda i, t: (i, 0, t)),
            pl.BlockSpec((1, gT.shape[1], tm), lambda i, t: (i, 0, t)),
            pl.BlockSpec((1, tm, d), lambda i, t: (i, t, 0)),
            _const_spec(wbfT.shape),
            _const_spec(wbmT.shape),
            _const_spec(woT.shape),
            _const_spec(gain.shape),
            _const_spec(w_gu.shape),
            _const_spec(w_down.shape),
        ],
        out_specs=pl.BlockSpec((1, tm, d), lambda i, t: (i, t, 0)),
        compiler_params=pltpu.CompilerParams(
            dimension_semantics=("arbitrary", "arbitrary"), vmem_limit_bytes=V7X_VMEM_LIMIT_BYTES),
        name="out_ffn",
    )(oT, gT, h3d, wbfT, wbmT, woT, gain, w_gu, w_down)


def _col(v):
    return v.astype(F32).reshape(-1, 1)


def _pick_tile(n, pref):
    t = min(n, pref)
    while n % t:
        t //= 2
    return t


def kernel(x, meta_tokens, ffn1_norm, ffn1_w_gu, ffn1_w_down, mix_norm, w_in, b_forget, b_gate, fox_q_norm,
           fox_k_norm, mla_cq_norm, mla_w_uq, mla_ckv_norm, mla_w_ukv, mla_q_norm, mla_k_norm, w_branch_fox,
           w_branch_mla, w_out, ffn2_norm, ffn2_w_gu, ffn2_w_down):
    b, s, d = x.shape
    depth = ffn1_norm.shape[0]
    if depth != 1:
        raise NotImplementedError("only depth 1 is supported")
    meta_rows = 128

    pos = jnp.arange(N_META + s, dtype=F32)
    inv_freq = ROPE_THETA ** (-jnp.arange(0, MLA_ROPE, 2, dtype=F32) / MLA_ROPE)
    ang = pos[:, None] * inv_freq[None, :]
    cosT, sinT = jnp.cos(ang).T, jnp.sin(ang).T
    pad = ((0, 0), (0, meta_rows - N_META))
    cos_meta, sin_meta = jnp.pad(cosT[:, :N_META], pad), jnp.pad(sinT[:, :N_META], pad)
    cos_real, sin_real = cosT[:, N_META:], sinT[:, N_META:]

    tm_fused = _pick_tile(s, 256)
    tm_proj = _pick_tile(s, 512)
    sub_proj = _pick_tile(tm_proj, 256)
    tq = _pick_tile(s, 256)

    w1gu, w1d = ffn1_w_gu[0].astype(BF16), ffn1_w_down[0].astype(BF16)
    w2gu, w2d = ffn2_w_gu[0].astype(BF16), ffn2_w_down[0].astype(BF16)
    g1, g2 = ffn1_norm[0].reshape(1, d).astype(F32), ffn2_norm[0].reshape(1, d).astype(F32)
    p = {
        "mix_norm": mix_norm[0].reshape(1, d).astype(F32),
        "w_lowT": w_in[0][:, OFF_FL:OFF_GATE].T.astype(BF16),
        "w_wideT": jnp.concatenate([w_in[0][:, :OFF_FL], w_in[0][:, OFF_GATE:]], axis=1).T.astype(BF16),
        "b_forget": _col(b_forget[0]),
        "b_gate": _col(b_gate[0]),
        "fox_q_norm": _col(fox_q_norm[0]),
        "fox_k_norm": _col(fox_k_norm[0]),
        "mla_cq_norm": _col(mla_cq_norm[0]),
        "mla_w_uqT": mla_w_uq[0].T.astype(BF16),
        "mla_ckv_norm": _col(mla_ckv_norm[0]),
        "mla_w_ukvT": mla_w_ukv[0].T.astype(BF16),
        "mla_q_norm": _col(mla_q_norm[0]),
        "mla_k_norm": _col(mla_k_norm[0]),
    }
    wbfT = w_branch_fox[0].T.astype(BF16)
    wbmT = w_branch_mla[0].T.astype(BF16)
    woT = w_out[0].astype(BF16)

    hm = jnp.pad(meta_tokens.astype(F32), ((0, meta_rows - N_META), (0, 0)))
    hm1 = _ffn(hm, g1, w1gu, w1d, tm=meta_rows)
    _, km, vmT, _ = _proj_meta(hm1[None], cos_meta, sin_meta, p)
    k_meta, vT_meta = km[0, :N_META], vmT[0, :, :N_META]

    h1, qT, k, vT, gT = _ffn_proj(x.astype(F32).reshape(b * s, d), g1, w1gu, w1d, cos_real, sin_real, p,
                                  b=b, s=s, tm=tm_fused)
    h1 = h1.reshape(b, s, d)
    oT = _attention(qT, k, vT, k_meta, vT_meta, tq=tq, group=ATTN_GROUP)
    h3 = _out_ffn(oT, gT, h1, wbfT, wbmT, woT, g2, w2gu, w2d, tm=tm_proj, sub=sub_proj)
    return h3.astype(x.dtype)
```

```python
import collections
import functools

import jax
import jax.numpy as jnp
from jax import lax
from jax.experimental import pallas as pl
from jax.experimental.pallas import tpu as pltpu

F32 = jnp.float32
BF16 = jnp.bfloat16

EPS = 1e-6
N_META = 16
FOX_HEADS = 8
FOX_DIM = 64
FOX_W = FOX_HEADS * FOX_DIM
MLA_HEADS = 8
MLA_Q_RANK = 256
MLA_KV_RANK = 128
MLA_NOPE = 64
MLA_ROPE = 32
MLA_QK = MLA_NOPE + MLA_ROPE
MLA_V = 64
ROPE_THETA = 10000.0
N_BRANCH = 2
LOG2E = 1.4426950408889634

N_HEADS = FOX_HEADS + MLA_HEADS
HEAD_PAD = 128
V_DIM = 64

OFF_FQ = 0
OFF_FK = OFF_FQ + FOX_W
OFF_FV = OFF_FK + FOX_W
OFF_FL = OFF_FV + FOX_W
OFF_CQ = OFF_FL + FOX_HEADS
OFF_CKV = OFF_CQ + MLA_Q_RANK
OFF_KR = OFF_CKV + MLA_KV_RANK
OFF_GATE = OFF_KR + MLA_ROPE
LOW_FL = 0
LOW_CQ = LOW_FL + FOX_HEADS
LOW_CKV = LOW_CQ + MLA_Q_RANK
LOW_KR = LOW_CKV + MLA_KV_RANK
LOW_ROWS = LOW_KR + MLA_ROPE
WIDE_FQ = 0
WIDE_FK = WIDE_FQ + FOX_W
WIDE_FV = WIDE_FK + FOX_W
WIDE_GATE = WIDE_FV + FOX_W

V7X_VMEM_LIMIT_BYTES = 56 * 1024 * 1024
MASK_VALUE = -1e30
ATTN_GROUP = 4
ATTN_LOOKAHEAD = 2
GATE_SPLIT_EIGHTHS = (0, 2, 4, 7, 8)
SHIFT_ROW = 104

_NT = (((1,), (1,)), ((), ()))
_TN = (((0,), (0,)), ((), ()))


def _const_spec(shape):
    zeros = (0,) * len(shape)
    return pl.BlockSpec(shape, lambda *_: zeros, pipeline_mode=pl.Buffered(1))


def _gate_up(x, gain, wgu_ref):
    d_ff = wgu_ref.shape[1] // 2
    ms = jnp.mean(x * x, axis=-1, keepdims=True)
    u = (x * lax.rsqrt(ms + EPS) * gain).astype(BF16)
    g = jnp.dot(u, wgu_ref[:, :d_ff], preferred_element_type=F32)
    up = jnp.dot(u, wgu_ref[:, d_ff:], preferred_element_type=F32)
    return g, up


def _half_ffn_out(x, g, up, wd_ref):
    a = (g * jax.nn.sigmoid(g) * up).astype(BF16)
    return x + 0.5 * jnp.dot(a, wd_ref[...], preferred_element_type=F32)


def _ffn_kernel(x_ref, gain_ref, wgu_ref, wd_ref, o_ref):
    x = x_ref[...]
    g, up = _gate_up(x, gain_ref[...], wgu_ref)
    o_ref[...] = _half_ffn_out(x, g, up, wd_ref)


def _ffn(x2d, gain, w_gu, w_down, *, tm):
    n, d = x2d.shape
    d_ff = w_down.shape[0]
    return pl.pallas_call(
        _ffn_kernel,
        out_shape=jax.ShapeDtypeStruct((n, d), F32),
        grid=(n // tm,),
        in_specs=[
            pl.BlockSpec((tm, d), lambda i: (i, 0)),
            _const_spec((1, d)),
            _const_spec((d, 2 * d_ff)),
            _const_spec((d_ff, d)),
        ],
        out_specs=pl.BlockSpec((tm, d), lambda i: (i, 0)),
        compiler_params=pltpu.CompilerParams(
            dimension_semantics=("arbitrary",), vmem_limit_bytes=V7X_VMEM_LIMIT_BYTES),
        name="ffn",
    )(x2d, gain, w_gu, w_down)


def _bf16_parts(c):
    hi = c.astype(BF16).astype(F32)
    r = c - hi
    mid = r.astype(BF16).astype(F32)
    lo = (r - mid).astype(BF16).astype(F32)
    return hi, mid, lo


def _rows8(vals, tm):
    row = lax.broadcasted_iota(jnp.int32, (8, tm), 0)
    out = jnp.zeros((8, tm), F32)
    for j, v in enumerate(vals):
        out = jnp.where(row == j, v, out)
    return out


def _key_tail(one_rows, first_row, tm):
    return jnp.concatenate([jnp.zeros((SHIFT_ROW - first_row, tm), F32), one_rows,
                            jnp.zeros((HEAD_PAD - SHIFT_ROW - 8, tm), F32)], axis=0)


def _rms_rows(x, n):
    return lax.rsqrt(jnp.sum(x * x, axis=0, keepdims=True) * (1.0 / n) + EPS)


def _rope_rows(x, cos, sin):
    x1 = x[:MLA_ROPE // 2]
    x2 = x[MLA_ROPE // 2:]
    return x1 * cos - x2 * sin, x1 * sin + x2 * cos


_ProjRefs = collections.namedtuple(
    "_ProjRefs", "gain wlowT wwideT bf bg gq gk gcq wuqT gckv wukvT gmq gmk cos sin qT k vT gT")


class _ProjOps:
    def __init__(self, u, r, tm, meta):
        self.u, self.r, self.tm, self.meta = u, r, tm, meta
        n_gate = r.wwideT.shape[0] - WIDE_GATE
        self.bounds = [WIDE_GATE + n_gate * e // 8 for e in GATE_SPLIT_EIGHTHS]

    def _wide_rows(self, r0, r1):
        return lax.dot_general(self.r.wwideT[r0:r1, :], self.u, _NT, preferred_element_type=F32)

    def low(self):
        return lax.dot_general(self.r.wlowT[...], self.u, _NT, preferred_element_type=F32)

    def fox(self):
        return self._wide_rows(0, WIDE_GATE)

    def gate_logits(self, j):
        return self._wide_rows(self.bounds[j], self.bounds[j + 1])

    def store_gates(self, j, logits):
        rows = slice(self.bounds[j] - WIDE_GATE, self.bounds[j + 1] - WIDE_GATE)
        self.r.gT[0, rows, :] = jax.nn.sigmoid(logits + self.r.bg[rows, :]).astype(BF16)

    def forget_cumsum(self, low, carry):
        tm = self.tm
        fl = low[LOW_FL:LOW_FL + FOX_HEADS] + self.r.bf[...]
        lf = jnp.minimum(fl, 0.0) - jnp.log1p(jnp.exp(-jnp.abs(fl)))
        r_i = lax.broadcasted_iota(jnp.int32, (tm, tm), 0)
        c_i = lax.broadcasted_iota(jnp.int32, (tm, tm), 1)
        upper = jnp.where(r_i <= c_i, 1.0, 0.0).astype(BF16)
        cs = None
        for part in _bf16_parts(lf):
            d = jnp.dot(part.astype(BF16), upper, preferred_element_type=F32)
            cs = d if cs is None else cs + d
        if self.meta:
            return (cs - cs[:, N_META - 1:N_META]) * LOG2E, None
        return (cs + carry) * LOG2E, carry + cs[:, tm - 1:tm]

    def mla_up(self, low):
        r = self.r
        cq = low[LOW_CQ:LOW_CQ + MLA_Q_RANK]
        cqn = (cq * _rms_rows(cq, MLA_Q_RANK) * r.gcq[...]).astype(BF16)
        qm = jnp.dot(r.wuqT[...], cqn, preferred_element_type=F32)
        ckv = low[LOW_CKV:LOW_CKV + MLA_KV_RANK]
        ckvn = (ckv * _rms_rows(ckv, MLA_KV_RANK) * r.gckv[...]).astype(BF16)
        kv = jnp.dot(r.wukvT[...], ckvn, preferred_element_type=F32)
        return qm, kv

    def store_mla_heads(self, low, qm, kv):
        r, tm = self.r, self.tm
        kr = low[LOW_KR:LOW_KR + MLA_ROPE]
        ss_kr = jnp.sum(kr * kr, axis=0, keepdims=True)
        cos = r.cos[...]
        sin = r.sin[...]
        gmq = r.gmq[...]
        gmk = r.gmk[...]
        zpad = jnp.zeros((HEAD_PAD - MLA_QK, tm), F32)
        ones = jnp.ones((1, tm), F32)
        ktail = _key_tail(_rows8([ones, ones, ones], tm), MLA_QK, tm)
        for hh in range(MLA_HEADS):
            qh = qm[hh * MLA_QK:(hh + 1) * MLA_QK]
            qn = qh * (_rms_rows(qh, MLA_QK) * (MLA_QK ** -0.5 * LOG2E)) * gmq
            q1, q2 = _rope_rows(qn[MLA_NOPE:], cos, sin)
            q_ext = jnp.concatenate([qn[:MLA_NOPE], q1, q2, zpad], axis=0)
            kn_raw = kv[hh * (MLA_NOPE + MLA_V):hh * (MLA_NOPE + MLA_V) + MLA_NOPE]
            rk = lax.rsqrt((jnp.sum(kn_raw * kn_raw, axis=0, keepdims=True) + ss_kr) * (1.0 / MLA_QK) + EPS)
            kn = kn_raw * rk * gmk[:MLA_NOPE]
            k1, k2 = _rope_rows(kr * rk * gmk[MLA_NOPE:], cos, sin)
            k_ext = jnp.concatenate([kn, k1, k2, ktail], axis=0)
            g = FOX_HEADS + hh
            r.qT[0, g * HEAD_PAD:(g + 1) * HEAD_PAD, :] = q_ext.astype(BF16)
            r.k[0, :, g * HEAD_PAD:(g + 1) * HEAD_PAD] = k_ext.T.astype(BF16)
            r.vT[0, g * V_DIM:(g + 1) * V_DIM, :] = (
                kv[hh * (MLA_NOPE + MLA_V) + MLA_NOPE:(hh + 1) * (MLA_NOPE + MLA_V)].astype(BF16))

    def store_fox_heads(self, fox, c):
        r, tm = self.r, self.tm
        ones = jnp.ones((1, tm), F32)
        one_rows = _rows8([ones, ones, ones], tm)
        zpad = jnp.zeros((HEAD_PAD - FOX_DIM - 16, tm), F32)
        ktail = _key_tail(one_rows, FOX_DIM + 16, tm)
        gq = r.gq[...]
        gk = r.gk[...]
        for hh in range(FOX_HEADS):
            qh = fox[WIDE_FQ + hh * FOX_DIM:WIDE_FQ + (hh + 1) * FOX_DIM]
            kh = fox[WIDE_FK + hh * FOX_DIM:WIDE_FK + (hh + 1) * FOX_DIM]
            qn = qh * (_rms_rows(qh, FOX_DIM) * (FOX_DIM ** -0.5 * LOG2E)) * gq
            kn = kh * _rms_rows(kh, FOX_DIM) * gk
            c_rows = _rows8(list(_bf16_parts(c[hh:hh + 1])), tm)
            q_ext = jnp.concatenate([qn, c_rows, one_rows, zpad], axis=0)
            k_ext = jnp.concatenate([kn, one_rows, -c_rows, ktail], axis=0)
            r.qT[0, hh * HEAD_PAD:(hh + 1) * HEAD_PAD, :] = q_ext.astype(BF16)
            r.k[0, :, hh * HEAD_PAD:(hh + 1) * HEAD_PAD] = k_ext.T.astype(BF16)
            r.vT[0, hh * V_DIM:(hh + 1) * V_DIM, :] = (
                fox[WIDE_FV + hh * FOX_DIM:WIDE_FV + (hh + 1) * FOX_DIM].astype(BF16))


def _rms_norm_bf16(x, gain):
    ms = jnp.mean(x * x, axis=-1, keepdims=True)
    return (x * lax.rsqrt(ms + EPS) * gain).astype(BF16)


def _proj_meta_kernel(h_ref, *refs):
    r = _ProjRefs(*refs)
    tm = h_ref.shape[1]
    ops = _ProjOps(_rms_norm_bf16(h_ref[0], r.gain[...]), r, tm, meta=True)
    low = ops.low()
    fox = ops.fox()
    c, _ = ops.forget_cumsum(low, None)
    qm, kv = ops.mla_up(low)
    ops.store_fox_heads(fox, c)
    ops.store_mla_heads(low, qm, kv)
    for j in range(len(ops.bounds) - 1):
        ops.store_gates(j, ops.gate_logits(j))


def _ffn_proj_kernel(x_ref, fgain_ref, wgu_ref, wd_ref, *refs, tiles_per_row):
    *proj_refs, h1_ref, qT_ref, k_ref, vT_ref, gT_ref, hprev_ref, carry_ref = refs
    r = _ProjRefs(*proj_refs, qT_ref, k_ref, vT_ref, gT_ref)
    t = pl.program_id(0)
    tm = x_ref.shape[0]

    @pl.when(t == 0)
    def _():
        hprev_ref[...] = jnp.zeros_like(hprev_ref)
        carry_ref[...] = jnp.zeros_like(carry_ref)

    ops = _ProjOps(_rms_norm_bf16(hprev_ref[...], r.gain[...]), r, tm, meta=False)
    first_of_row = lax.rem(t + (tiles_per_row - 1), tiles_per_row) == 0
    carry = jnp.where(first_of_row, 0.0, carry_ref[:, 0:1])
    x = x_ref[...]

    low = ops.low()
    g, up = _gate_up(x, fgain_ref[...], wgu_ref)
    c, carry = ops.forget_cumsum(low, carry)
    qm, kv = ops.mla_up(low)
    fox = ops.fox()
    ops.store_mla_heads(low, qm, kv)
    h1 = _half_ffn_out(x, g, up, wd_ref)
    ops.store_fox_heads(fox, c)
    n_chunks = len(ops.bounds) - 1
    gate = ops.gate_logits(0)
    for j in range(n_chunks):
        gate_next = ops.gate_logits(j + 1) if j + 1 < n_chunks else None
        ops.store_gates(j, gate)
        gate = gate_next
    h1_ref[...] = h1
    hprev_ref[...] = h1
    carry_ref[...] = jnp.broadcast_to(carry, carry_ref.shape)


def _proj_consts(p):
    return [p["mix_norm"], p["w_lowT"], p["w_wideT"], p["b_forget"], p["b_gate"], p["fox_q_norm"],
            p["fox_k_norm"], p["mla_cq_norm"], p["mla_w_uqT"], p["mla_ckv_norm"], p["mla_w_ukvT"],
            p["mla_q_norm"], p["mla_k_norm"]]


def _proj_out_shapes(b, s, n_gate):
    return (
        jax.ShapeDtypeStruct((b, N_HEADS * HEAD_PAD, s), BF16),
        jax.ShapeDtypeStruct((b, s, N_HEADS * HEAD_PAD), BF16),
        jax.ShapeDtypeStruct((b, N_HEADS * V_DIM, s), BF16),
        jax.ShapeDtypeStruct((b, n_gate, s), BF16),
    )


def _proj_meta(h3d, cosT, sinT, p):
    _, tm, d = h3d.shape
    n_gate = p["w_wideT"].shape[0] - WIDE_GATE
    consts = _proj_consts(p)
    whole = lambda shape: pl.BlockSpec(shape, lambda i: (0,) * len(shape))
    return pl.pallas_call(
        _proj_meta_kernel,
        out_shape=_proj_out_shapes(1, tm, n_gate),
        grid=(1,),
        in_specs=[whole(h3d.shape)] + [whole(c.shape) for c in consts] + [whole(cosT.shape), whole(sinT.shape)],
        out_specs=tuple(whole(o.shape) for o in _proj_out_shapes(1, tm, n_gate)),
        compiler_params=pltpu.CompilerParams(
            dimension_semantics=("arbitrary",), vmem_limit_bytes=V7X_VMEM_LIMIT_BYTES),
        name="proj_meta",
    )(h3d, *consts, cosT, sinT)


def _ffn_proj(x2d, fgain, w_gu, w_down, cosT, sinT, p, *, b, s, tm):
    n, d = x2d.shape
    n_gate = p["w_wideT"].shape[0] - WIDE_GATE
    consts = _proj_consts(p)
    tiles_per_row = s // tm
    n_tiles = n // tm

    def cur(t):
        return jnp.minimum(t, n_tiles - 1)

    def prev_row(t):
        tt = jnp.maximum(t - 1, 0)
        return tt // tiles_per_row, tt % tiles_per_row

    outs = pl.pallas_call(
        functools.partial(_ffn_proj_kernel, tiles_per_row=tiles_per_row),
        out_shape=(jax.ShapeDtypeStruct((n, d), F32),) + _proj_out_shapes(b, s, n_gate),
        grid=(n_tiles + 1,),
        in_specs=[
            pl.BlockSpec((tm, d), lambda t: (cur(t), 0)),
            _const_spec(fgain.shape),
            _const_spec(w_gu.shape),
            _const_spec(w_down.shape),
        ] + [_const_spec(c.shape) for c in consts]
        + [pl.BlockSpec((MLA_ROPE // 2, tm), lambda t: (0, prev_row(t)[1]))] * 2,
        out_specs=(
            pl.BlockSpec((tm, d), lambda t: (cur(t), 0)),
            pl.BlockSpec((1, N_HEADS * HEAD_PAD, tm), lambda t: (prev_row(t)[0], 0, prev_row(t)[1])),
            pl.BlockSpec((1, tm, N_HEADS * HEAD_PAD), lambda t: (prev_row(t)[0], prev_row(t)[1], 0)),
            pl.BlockSpec((1, N_HEADS * V_DIM, tm), lambda t: (prev_row(t)[0], 0, prev_row(t)[1])),
            pl.BlockSpec((1, n_gate, tm), lambda t: (prev_row(t)[0], 0, prev_row(t)[1])),
        ),
        scratch_shapes=[pltpu.VMEM((tm, d), F32), pltpu.VMEM((FOX_HEADS, 128), F32)],
        compiler_params=pltpu.CompilerParams(
            dimension_semantics=("arbitrary",), vmem_limit_bytes=V7X_VMEM_LIMIT_BYTES),
        name="ffn_proj",
    )(x2d, fgain, w_gu, w_down, *consts, cosT, sinT)
    return outs


def _attn_kernel(qT_ref, k_ref, vT_ref, km_ref, vmT_ref, o_ref, *, tq, group):
    s_len = qT_ref.shape[2]
    row = lax.broadcasted_iota(jnp.int32, (tq, tq), 0)
    col = lax.broadcasted_iota(jnp.int32, (tq, tq), 1)
    causal = row <= col
    both = lax.broadcasted_iota(jnp.int32, (tq, 2 * tq), 1)
    row2 = lax.broadcasted_iota(jnp.int32, (tq, 2 * tq), 0)
    causal_pair = (row2 <= both) | (both >= tq)
    ones_real = jnp.ones((16, s_len), BF16)
    ones_meta = jnp.ones((16, N_META), BF16)
    v_ext = [jnp.concatenate([vT_ref[0, g * V_DIM:(g + 1) * V_DIM, :], ones_real], axis=0)
             for g in range(group)]
    vm_ext = [jnp.concatenate([vmT_ref[g * V_DIM:(g + 1) * V_DIM, :], ones_meta], axis=0)
              for g in range(group)]
    q_tail_zeros = jnp.zeros((HEAD_PAD - SHIFT_ROW - 8, 2 * tq), F32)
    q_head_zeros = jnp.zeros((SHIFT_ROW - MLA_QK, 2 * tq), F32)

    def meta_scores(i, g, exact):
        head = slice(g * HEAD_PAD, (g + 1) * HEAD_PAD)
        qT = qT_ref[0, head, 2 * i * tq:(2 * i + 2) * tq]
        sm = jnp.dot(km_ref[:, head], qT, preferred_element_type=F32)
        shift = None
        if not exact:
            shift = jnp.max(sm, axis=0, keepdims=True)
            parts = _rows8([-part for part in _bf16_parts(shift)], 2 * tq)
            tail = jnp.concatenate([q_head_zeros, parts, q_tail_zeros], axis=0).astype(BF16)
            qT = jnp.concatenate([qT[:MLA_QK], tail], axis=0)
        return qT, sm, shift

    def scores(i, g, qT, sm, shift):
        head = slice(g * HEAD_PAD, (g + 1) * HEAD_PAD)
        n1 = (2 * i + 1) * tq
        s_main = jnp.dot(k_ref[0, :n1, head], qT, preferred_element_type=F32)
        s_low = jnp.dot(k_ref[0, n1:n1 + tq, head], qT[:, tq:], preferred_element_type=F32)
        return s_main, s_low, sm, shift

    def finish(i, g, s_main, s_low, sm, shift, exact):
        n1 = (2 * i + 1) * tq
        diag = jnp.where(causal_pair, s_main[n1 - tq:], MASK_VALUE)
        low = jnp.where(causal, s_low, MASK_VALUE)
        top = s_main[:n1 - tq] if i > 0 else None
        if exact:
            m = jnp.maximum(jnp.max(diag, axis=0, keepdims=True), jnp.max(sm, axis=0, keepdims=True))
            if i > 0:
                m = jnp.maximum(m, jnp.max(top, axis=0, keepdims=True))
            m = jnp.concatenate([m[:, :tq], jnp.maximum(m[:, tq:], jnp.max(low, axis=0, keepdims=True))],
                                axis=1)
            diag, low, sm = diag - m, low - m[:, tq:], sm - m
            top = top - m if i > 0 else None
        else:
            sm = sm - shift
        parts = [jnp.exp2(diag).astype(BF16)]
        if i > 0:
            parts = [jnp.exp2(top).astype(BF16)] + parts
        p = jnp.concatenate(parts, axis=0) if len(parts) > 1 else parts[0]
        p_low = jnp.exp2(low).astype(BF16)
        pm = jnp.exp2(sm).astype(BF16)
        o = (jnp.dot(v_ext[g][:, :n1], p, preferred_element_type=F32)
             + jnp.dot(vm_ext[g], pm, preferred_element_type=F32))
        o_hi = o[:, tq:] + jnp.dot(v_ext[g][:, n1:n1 + tq], p_low, preferred_element_type=F32)
        den = jnp.concatenate([o[V_DIM:V_DIM + 1, :tq], o_hi[V_DIM:V_DIM + 1]], axis=1)
        rows = slice(g * V_DIM, (g + 1) * V_DIM)
        o_ref[0, rows, 2 * i * tq:(2 * i + 1) * tq] = (o[:V_DIM, :tq] * (1.0 / den[:, :tq])).astype(BF16)
        o_ref[0, rows, (2 * i + 1) * tq:(2 * i + 2) * tq] = (o_hi[:V_DIM] * (1.0 / den[:, tq:])).astype(BF16)
        usable = (den > 0.0) & (den < jnp.inf)
        return jnp.where(usable, 0.0, 1.0)

    nq = s_len // (2 * tq)
    order = [t for pair in zip(range(nq - 1, -1, -1), range(nq)) for t in pair][:nq]

    def run(exact):
        flags = jnp.zeros((1, 2 * tq), F32)
        meta = {(i, g): meta_scores(i, g, exact) for i in order for g in range(group)}
        pending = []
        for i in order:
            for g in range(group):
                pending.append((i, g) + scores(i, g, *meta[i, g]))
                if len(pending) > ATTN_LOOKAHEAD:
                    flags = jnp.maximum(flags, finish(*pending.pop(0), exact))
        for unit in pending:
            flags = jnp.maximum(flags, finish(*unit, exact))
        return flags

    flags = run(exact=False)

    @pl.when(jnp.max(flags) > 0.0)
    def _():
        run(exact=True)


def _attention(qT, k, vT, k_meta, vT_meta, *, tq, group):
    b, _, s = qT.shape
    return pl.pallas_call(
        functools.partial(_attn_kernel, tq=tq, group=group),
        out_shape=jax.ShapeDtypeStruct((b, N_HEADS * V_DIM, s), BF16),
        grid=(b, N_HEADS // group),
        in_specs=[
            pl.BlockSpec((1, group * HEAD_PAD, s), lambda i, h: (i, h, 0)),
            pl.BlockSpec((1, s, group * HEAD_PAD), lambda i, h: (i, 0, h)),
            pl.BlockSpec((1, group * V_DIM, s), lambda i, h: (i, h, 0)),
            pl.BlockSpec((N_META, group * HEAD_PAD), lambda i, h: (0, h)),
            pl.BlockSpec((group * V_DIM, N_META), lambda i, h: (h, 0)),
        ],
        out_specs=pl.BlockSpec((1, group * V_DIM, s), lambda i, h: (i, h, 0)),
        compiler_params=pltpu.CompilerParams(
            dimension_semantics=("arbitrary", "arbitrary"), vmem_limit_bytes=V7X_VMEM_LIMIT_BYTES),
        name="attention",
    )(qT, k, vT, k_meta, vT_meta)


def _out_ffn_kernel(oT_ref, gT_ref, h_ref, wbfT_ref, wbmT_ref, woT_ref, gain_ref, wgu_ref, wd_ref, o_ref, *, sub):
    tm = h_ref.shape[1]
    d = woT_ref.shape[0]
    n_sub = tm // sub

    def branches(j):
        lanes = slice(j * sub, (j + 1) * sub)
        yf = jnp.dot(wbfT_ref[...], oT_ref[0, :FOX_W, lanes], preferred_element_type=F32)
        ym = jnp.dot(wbmT_ref[...], oT_ref[0, FOX_W:, lanes], preferred_element_type=F32)
        gf = gT_ref[0, :d, lanes].astype(F32)
        gm = gT_ref[0, d:, lanes].astype(F32)
        return (gf * yf + gm * ym).astype(BF16)

    def mixed(j, z):
        mix = lax.dot_general(z, woT_ref[...], _TN, preferred_element_type=F32)
        return h_ref[0, j * sub:(j + 1) * sub, :] + mix

    zs = [branches(j) for j in range(n_sub)]
    h2 = [mixed(j, zs[j]) for j in range(n_sub)]
    gu = [_gate_up(h2[j], gain_ref[...], wgu_ref) for j in range(n_sub)]
    for j in range(n_sub):
        o_ref[0, j * sub:(j + 1) * sub, :] = _half_ffn_out(h2[j], *gu[j], wd_ref)


def _out_ffn(oT, gT, h3d, wbfT, wbmT, woT, gain, w_gu, w_down, *, tm, sub):
    b, s, d = h3d.shape
    return pl.pallas_call(
        functools.partial(_out_ffn_kernel, sub=sub),
        out_shape=jax.ShapeDtypeStruct((b, s, d), F32),
        grid=(b, s // tm),
        in_specs=[
            pl.BlockSpec((1, oT.shape[1], tm), lambda i, t: (i, 0, t)),
            pl.BlockSpec((1, gT.shape[1], tm), lambda i, t: (i, 0, t)),
            pl.BlockSpec((1, tm, d), lambda i, t: (i, t, 0)),
            _const_spec(wbfT.shape),
            _const_spec(wbmT.shape),
            _const_spec(woT.shape),
            _const_spec(gain.shape),
            _const_spec(w_gu.shape),
            _const_spec(w_down.shape),
        ],
        out_specs=pl.BlockSpec((1, tm, d), lambda i, t: (i, t, 0)),
        compiler_params=pltpu.CompilerParams(
            dimension_semantics=("arbitrary", "arbitrary"), vmem_limit_bytes=V7X_VMEM_LIMIT_BYTES),
        name="out_ffn",
    )(oT, gT, h3d, wbfT, wbmT, woT, gain, w_gu, w_down)


def _col(v):
    return v.astype(F32).reshape(-1, 1)


def _pick_tile(n, pref):
    t = min(n, pref)
    while n % t:
        t //= 2
    return t


def kernel(x, meta_tokens, ffn1_norm, ffn1_w_gu, ffn1_w_down, mix_norm, w_in, b_forget, b_gate, fox_q_norm,
           fox_k_norm, mla_cq_norm, mla_w_uq, mla_ckv_norm, mla_w_ukv, mla_q_norm, mla_k_norm, w_branch_fox,
           w_branch_mla, w_out, ffn2_norm, ffn2_w_gu, ffn2_w_down):
    b, s, d = x.shape
    depth = ffn1_norm.shape[0]
    if depth != 1:
        raise NotImplementedError("only depth 1 is supported")
    meta_rows = 128

    pos = jnp.arange(N_META + s, dtype=F32)
    inv_freq = ROPE_THETA ** (-jnp.arange(0, MLA_ROPE, 2, dtype=F32) / MLA_ROPE)
    ang = pos[:, None] * inv_freq[None, :]
    cosT, sinT = jnp.cos(ang).T, jnp.sin(ang).T
    pad = ((0, 0), (0, meta_rows - N_META))
    cos_meta, sin_meta = jnp.pad(cosT[:, :N_META], pad), jnp.pad(sinT[:, :N_META], pad)
    cos_real, sin_real = cosT[:, N_META:], sinT[:, N_META:]

    tm_fused = _pick_tile(s, 256)
    tm_proj = _pick_tile(s, 512)
    sub_proj = _pick_tile(tm_proj, 256)
    tq = _pick_tile(s, 256)

    w1gu, w1d = ffn1_w_gu[0].astype(BF16), ffn1_w_down[0].astype(BF16)
    w2gu, w2d = ffn2_w_gu[0].astype(BF16), ffn2_w_down[0].astype(BF16)
    g1, g2 = ffn1_norm[0].reshape(1, d).astype(F32), ffn2_norm[0].reshape(1, d).astype(F32)
    p = {
        "mix_norm": mix_norm[0].reshape(1, d).astype(F32),
        "w_lowT": w_in[0][:, OFF_FL:OFF_GATE].T.astype(BF16),
        "w_wideT": jnp.concatenate([w_in[0][:, :OFF_FL], w_in[0][:, OFF_GATE:]], axis=1).T.astype(BF16),
        "b_forget": _col(b_forget[0]),
        "b_gate": _col(b_gate[0]),
        "fox_q_norm": _col(fox_q_norm[0]),
        "fox_k_norm": _col(fox_k_norm[0]),
        "mla_cq_norm": _col(mla_cq_norm[0]),
        "mla_w_uqT": mla_w_uq[0].T.astype(BF16),
        "mla_ckv_norm": _col(mla_ckv_norm[0]),
        "mla_w_ukvT": mla_w_ukv[0].T.astype(BF16),
        "mla_q_norm": _col(mla_q_norm[0]),
        "mla_k_norm": _col(mla_k_norm[0]),
    }
    wbfT = w_branch_fox[0].T.astype(BF16)
    wbmT = w_branch_mla[0].T.astype(BF16)
    woT = w_out[0].astype(BF16)

    hm = jnp.pad(meta_tokens.astype(F32), ((0, meta_rows - N_META), (0, 0)))
    hm1 = _ffn(hm, g1, w1gu, w1d, tm=meta_rows)
    _, km, vmT, _ = _proj_meta(hm1[None], cos_meta, sin_meta, p)
    k_meta, vT_meta = km[0, :N_META], vmT[0, :, :N_META]

    h1, qT, k, vT, gT = _ffn_proj(x.astype(F32).reshape(b * s, d), g1, w1gu, w1d, cos_real, sin_real, p,
                                  b=b, s=s, tm=tm_fused)
    h1 = h1.reshape(b, s, d)
    oT = _attention(qT, k, vT, k_meta, vT_meta, tq=tq, group=ATTN_GROUP)
    h3 = _out_ffn(oT, gT, h1, wbfT, wbmT, woT, g2, w2gu, w2d, tm=tm_proj, sub=sub_proj)
    return h3.astype(x.dtype)
```

```python
import collections
import functools

import jax
import jax.numpy as jnp
from jax import lax
from jax.experimental import pallas as pl
from jax.experimental.pallas import tpu as pltpu

F32 = jnp.float32
BF16 = jnp.bfloat16

EPS = 1e-6
N_META = 16
FOX_HEADS = 8
FOX_DIM = 64
FOX_W = FOX_HEADS * FOX_DIM
MLA_HEADS = 8
MLA_Q_RANK = 256
MLA_KV_RANK = 128
MLA_NOPE = 64
MLA_ROPE = 32
MLA_QK = MLA_NOPE + MLA_ROPE
MLA_V = 64
ROPE_THETA = 10000.0
N_BRANCH = 2
LOG2E = 1.4426950408889634

N_HEADS = FOX_HEADS + MLA_HEADS
HEAD_PAD = 128
V_DIM = 64

OFF_FQ = 0
OFF_FK = OFF_FQ + FOX_W
OFF_FV = OFF_FK + FOX_W
OFF_FL = OFF_FV + FOX_W
OFF_CQ = OFF_FL + FOX_HEADS
OFF_CKV = OFF_CQ + MLA_Q_RANK
OFF_KR = OFF_CKV + MLA_KV_RANK
OFF_GATE = OFF_KR + MLA_ROPE
LOW_FL = 0
LOW_CQ = LOW_FL + FOX_HEADS
LOW_CKV = LOW_CQ + MLA_Q_RANK
LOW_KR = LOW_CKV + MLA_KV_RANK
LOW_ROWS = LOW_KR + MLA_ROPE
WIDE_FQ = 0
WIDE_FK = WIDE_FQ + FOX_W
WIDE_FV = WIDE_FK + FOX_W
WIDE_GATE = WIDE_FV + FOX_W

V7X_VMEM_LIMIT_BYTES = 56 * 1024 * 1024
MASK_VALUE = -1e30
ATTN_GROUP = 4
ATTN_LOOKAHEAD = 2
GATE_SPLIT_EIGHTHS = (0, 2, 4, 7, 8)
SHIFT_ROW = 104

_NT = (((1,), (1,)), ((), ()))
_TN = (((0,), (0,)), ((), ()))


def _const_spec(shape):
    zeros = (0,) * len(shape)
    return pl.BlockSpec(shape, lambda *_: zeros, pipeline_mode=pl.Buffered(1))


def _gate_up(x, gain, wgu_ref):
    d_ff = wgu_ref.shape[1] // 2
    ms = jnp.mean(x * x, axis=-1, keepdims=True)
    u = (x * lax.rsqrt(ms + EPS) * gain).astype(BF16)
    g = jnp.dot(u, wgu_ref[:, :d_ff], preferred_element_type=F32)
    up = jnp.dot(u, wgu_ref[:, d_ff:], preferred_element_type=F32)
    return g, up


def _half_ffn_out(x, g, up, wd_ref):
    a = (g * jax.nn.sigmoid(g) * up).astype(BF16)
    return x + 0.5 * jnp.dot(a, wd_ref[...], preferred_element_type=F32)


def _ffn_kernel(x_ref, gain_ref, wgu_ref, wd_ref, o_ref):
    x = x_ref[...]
    g, up = _gate_up(x, gain_ref[...], wgu_ref)
    o_ref[...] = _half_ffn_out(x, g, up, wd_ref)


def _ffn(x2d, gain, w_gu, w_down, *, tm):
    n, d = x2d.shape
    d_ff = w_down.shape[0]
    return pl.pallas_call(
        _ffn_kernel,
        out_shape=jax.ShapeDtypeStruct((n, d), F32),
        grid=(n // tm,),
        in_specs=[
            pl.BlockSpec((tm, d), lambda i: (i, 0)),
            _const_spec((1, d)),
            _const_spec((d, 2 * d_ff)),
            _const_spec((d_ff, d)),
        ],
        out_specs=pl.BlockSpec((tm, d), lambda i: (i, 0)),
        compiler_params=pltpu.CompilerParams(
            dimension_semantics=("arbitrary",), vmem_limit_bytes=V7X_VMEM_LIMIT_BYTES),
        name="ffn",
    )(x2d, gain, w_gu, w_down)


def _bf16_parts(c):
    hi = c.astype(BF16).astype(F32)
    r = c - hi
    mid = r.astype(BF16).astype(F32)
    lo = (r - mid).astype(BF16).astype(F32)
    return hi, mid, lo


def _rows8(vals, tm):
    row = lax.broadcasted_iota(jnp.int32, (8, tm), 0)
    out = jnp.zeros((8, tm), F32)
    for j, v in enumerate(vals):
        out = jnp.where(row == j, v, out)
    return out


def _key_tail(one_rows, first_row, tm):
    return jnp.concatenate([jnp.zeros((SHIFT_ROW - first_row, tm), F32), one_rows,
                            jnp.zeros((HEAD_PAD - SHIFT_ROW - 8, tm), F32)], axis=0)


def _rms_rows(x, n):
    return lax.rsqrt(jnp.sum(x * x, axis=0, keepdims=True) * (1.0 / n) + EPS)


def _rope_rows(x, cos, sin):
    x1 = x[:MLA_ROPE // 2]
    x2 = x[MLA_ROPE // 2:]
    return x1 * cos - x2 * sin, x1 * sin + x2 * cos


_ProjRefs = collections.namedtuple(
    "_ProjRefs", "gain wlowT wwideT bf bg gq gk gcq wuqT gckv wukvT gmq gmk cos sin qT k vT gT dg")


class _ProjOps:
    def __init__(self, u, r, tm, meta):
        self.u, self.r, self.tm, self.meta = u, r, tm, meta
        n_gate = r.wwideT.shape[0] - WIDE_GATE
        self.bounds = [WIDE_GATE + n_gate * e // 8 for e in GATE_SPLIT_EIGHTHS]

    def _wide_rows(self, r0, r1):
        return lax.dot_general(self.r.wwideT[r0:r1, :], self.u, _NT, preferred_element_type=F32)

    def low(self):
        return lax.dot_general(self.r.wlowT[...], self.u, _NT, preferred_element_type=F32)

    def fox(self):
        return self._wide_rows(0, WIDE_GATE)

    def gate_logits(self, j):
        return self._wide_rows(self.bounds[j], self.bounds[j + 1])

    def store_gates(self, j, logits):
        rows = slice(self.bounds[j] - WIDE_GATE, self.bounds[j + 1] - WIDE_GATE)
        self.r.gT[0, rows, :] = jax.nn.sigmoid(logits + self.r.bg[rows, :]).astype(BF16)

    def forget_cumsum(self, low, carry):
        tm = self.tm
        fl = low[LOW_FL:LOW_FL + FOX_HEADS] + self.r.bf[...]
        lf = jnp.minimum(fl, 0.0) - jnp.log1p(jnp.exp(-jnp.abs(fl)))
        r_i = lax.broadcasted_iota(jnp.int32, (tm, tm), 0)
        c_i = lax.broadcasted_iota(jnp.int32, (tm, tm), 1)
        upper = jnp.where(r_i <= c_i, 1.0, 0.0).astype(BF16)
        cs = None
        for part in _bf16_parts(lf):
            d = jnp.dot(part.astype(BF16), upper, preferred_element_type=F32)
            cs = d if cs is None else cs + d
        if self.meta:
            return (cs - cs[:, N_META - 1:N_META]) * LOG2E, None
        return (cs + carry) * LOG2E, carry + cs[:, tm - 1:tm]

    def mla_up(self, low):
        r = self.r
        cq = low[LOW_CQ:LOW_CQ + MLA_Q_RANK]
        cqn = (cq * _rms_rows(cq, MLA_Q_RANK) * r.gcq[...]).astype(BF16)
        qm = jnp.dot(r.wuqT[...], cqn, preferred_element_type=F32)
        ckv = low[LOW_CKV:LOW_CKV + MLA_KV_RANK]
        ckvn = (ckv * _rms_rows(ckv, MLA_KV_RANK) * r.gckv[...]).astype(BF16)
        kv = jnp.dot(r.wukvT[...], ckvn, preferred_element_type=F32)
        return qm, kv

    def store_mla_heads(self, low, qm, kv):
        r, tm = self.r, self.tm
        kr = low[LOW_KR:LOW_KR + MLA_ROPE]
        ss_kr = jnp.sum(kr * kr, axis=0, keepdims=True)
        cos = r.cos[...]
        sin = r.sin[...]
        gmq = r.gmq[...]
        gmk = r.gmk[...]
        zpad = jnp.zeros((HEAD_PAD - MLA_QK, tm), F32)
        ones = jnp.ones((1, tm), F32)
        ktail = _key_tail(_rows8([ones, ones, ones], tm), MLA_QK, tm)
        for hh in range(MLA_HEADS):
            qh = qm[hh * MLA_QK:(hh + 1) * MLA_QK]
            qn = qh * (_rms_rows(qh, MLA_QK) * (MLA_QK ** -0.5 * LOG2E)) * gmq
            q1, q2 = _rope_rows(qn[MLA_NOPE:], cos, sin)
            q_ext = jnp.concatenate([qn[:MLA_NOPE], q1, q2, zpad], axis=0)
            kn_raw = kv[hh * (MLA_NOPE + MLA_V):hh * (MLA_NOPE + MLA_V) + MLA_NOPE]
            rk = lax.rsqrt((jnp.sum(kn_raw * kn_raw, axis=0, keepdims=True) + ss_kr) * (1.0 / MLA_QK) + EPS)
            kn = kn_raw * rk * gmk[:MLA_NOPE]
            k1, k2 = _rope_rows(kr * rk * gmk[MLA_NOPE:], cos, sin)
            k_ext = jnp.concatenate([kn, k1, k2, ktail], axis=0)
            g = FOX_HEADS + hh
            r.qT[0, g * HEAD_PAD:(g + 1) * HEAD_PAD, :] = q_ext.astype(BF16)
            r.k[0, :, g * HEAD_PAD:(g + 1) * HEAD_PAD] = k_ext.T.astype(BF16)
            r.dg[0, g] = jnp.sum(q_ext * k_ext, axis=0, keepdims=True)
            r.vT[0, g * V_DIM:(g + 1) * V_DIM, :] = (
                kv[hh * (MLA_NOPE + MLA_V) + MLA_NOPE:(hh + 1) * (MLA_NOPE + MLA_V)].astype(BF16))

    def store_fox_heads(self, fox, c):
        r, tm = self.r, self.tm
        ones = jnp.ones((1, tm), F32)
        one_rows = _rows8([ones, ones, ones], tm)
        zpad = jnp.zeros((HEAD_PAD - FOX_DIM - 16, tm), F32)
        ktail = _key_tail(one_rows, FOX_DIM + 16, tm)
        gq = r.gq[...]
        gk = r.gk[...]
        for hh in range(FOX_HEADS):
            qh = fox[WIDE_FQ + hh * FOX_DIM:WIDE_FQ + (hh + 1) * FOX_DIM]
            kh = fox[WIDE_FK + hh * FOX_DIM:WIDE_FK + (hh + 1) * FOX_DIM]
            qn = qh * (_rms_rows(qh, FOX_DIM) * (FOX_DIM ** -0.5 * LOG2E)) * gq
            kn = kh * _rms_rows(kh, FOX_DIM) * gk
            c_rows = _rows8(list(_bf16_parts(c[hh:hh + 1])), tm)
            q_ext = jnp.concatenate([qn, c_rows, one_rows, zpad], axis=0)
            k_ext = jnp.concatenate([kn, one_rows, -c_rows, ktail], axis=0)
            r.qT[0, hh * HEAD_PAD:(hh + 1) * HEAD_PAD, :] = q_ext.astype(BF16)
            r.k[0, :, hh * HEAD_PAD:(hh + 1) * HEAD_PAD] = k_ext.T.astype(BF16)
            r.dg[0, hh] = jnp.sum(q_ext * k_ext, axis=0, keepdims=True)
            r.vT[0, hh * V_DIM:(hh + 1) * V_DIM, :] = (
                fox[WIDE_FV + hh * FOX_DIM:WIDE_FV + (hh + 1) * FOX_DIM].astype(BF16))


def _rms_norm_bf16(x, gain):
    ms = jnp.mean(x * x, axis=-1, keepdims=True)
    return (x * lax.rsqrt(ms + EPS) * gain).astype(BF16)


def _proj_meta_kernel(h_ref, *refs):
    r = _ProjRefs(*refs)
    tm = h_ref.shape[1]
    ops = _ProjOps(_rms_norm_bf16(h_ref[0], r.gain[...]), r, tm, meta=True)
    low = ops.low()
    fox = ops.fox()
    c, _ = ops.forget_cumsum(low, None)
    qm, kv = ops.mla_up(low)
    ops.store_fox_heads(fox, c)
    ops.store_mla_heads(low, qm, kv)
    for j in range(len(ops.bounds) - 1):
        ops.store_gates(j, ops.gate_logits(j))


def _ffn_proj_kernel(x_ref, fgain_ref, wgu_ref, wd_ref, *refs, tiles_per_row):
    *proj_refs, h1_ref, qT_ref, k_ref, vT_ref, gT_ref, dg_ref, hprev_ref, carry_ref = refs
    r = _ProjRefs(*proj_refs, qT_ref, k_ref, vT_ref, gT_ref, dg_ref)
    t = pl.program_id(0)
    tm = x_ref.shape[0]

    @pl.when(t == 0)
    def _():
        hprev_ref[...] = jnp.zeros_like(hprev_ref)
        carry_ref[...] = jnp.zeros_like(carry_ref)

    ops = _ProjOps(_rms_norm_bf16(hprev_ref[...], r.gain[...]), r, tm, meta=False)
    first_of_row = lax.rem(t + (tiles_per_row - 1), tiles_per_row) == 0
    carry = jnp.where(first_of_row, 0.0, carry_ref[:, 0:1])
    x = x_ref[...]

    low = ops.low()
    g, up = _gate_up(x, fgain_ref[...], wgu_ref)
    c, carry = ops.forget_cumsum(low, carry)
    qm, kv = ops.mla_up(low)
    fox = ops.fox()
    ops.store_mla_heads(low, qm, kv)
    h1 = _half_ffn_out(x, g, up, wd_ref)
    ops.store_fox_heads(fox, c)
    n_chunks = len(ops.bounds) - 1
    gate = ops.gate_logits(0)
    for j in range(n_chunks):
        gate_next = ops.gate_logits(j + 1) if j + 1 < n_chunks else None
        ops.store_gates(j, gate)
        gate = gate_next
    h1_ref[...] = h1
    hprev_ref[...] = h1
    carry_ref[...] = jnp.broadcast_to(carry, carry_ref.shape)


def _proj_consts(p):
    return [p["mix_norm"], p["w_lowT"], p["w_wideT"], p["b_forget"], p["b_gate"], p["fox_q_norm"],
            p["fox_k_norm"], p["mla_cq_norm"], p["mla_w_uqT"], p["mla_ckv_norm"], p["mla_w_ukvT"],
            p["mla_q_norm"], p["mla_k_norm"]]


def _proj_out_shapes(b, s, n_gate):
    return (
        jax.ShapeDtypeStruct((b, N_HEADS * HEAD_PAD, s), BF16),
        jax.ShapeDtypeStruct((b, s, N_HEADS * HEAD_PAD), BF16),
        jax.ShapeDtypeStruct((b, N_HEADS * V_DIM, s), BF16),
        jax.ShapeDtypeStruct((b, n_gate, s), BF16),
        jax.ShapeDtypeStruct((b, N_HEADS, 1, s), F32),
    )


def _proj_meta(h3d, cosT, sinT, p):
    _, tm, d = h3d.shape
    n_gate = p["w_wideT"].shape[0] - WIDE_GATE
    consts = _proj_consts(p)
    whole = lambda shape: pl.BlockSpec(shape, lambda i: (0,) * len(shape))
    return pl.pallas_call(
        _proj_meta_kernel,
        out_shape=_proj_out_shapes(1, tm, n_gate),
        grid=(1,),
        in_specs=[whole(h3d.shape)] + [whole(c.shape) for c in consts] + [whole(cosT.shape), whole(sinT.shape)],
        out_specs=tuple(whole(o.shape) for o in _proj_out_shapes(1, tm, n_gate)),
        compiler_params=pltpu.CompilerParams(
            dimension_semantics=("arbitrary",), vmem_limit_bytes=V7X_VMEM_LIMIT_BYTES),
        name="proj_meta",
    )(h3d, *consts, cosT, sinT)


def _ffn_proj(x2d, fgain, w_gu, w_down, cosT, sinT, p, *, b, s, tm):
    n, d = x2d.shape
    n_gate = p["w_wideT"].shape[0] - WIDE_GATE
    consts = _proj_consts(p)
    tiles_per_row = s // tm
    n_tiles = n // tm

    def cur(t):
        return jnp.minimum(t, n_tiles - 1)

    def prev_row(t):
        tt = jnp.maximum(t - 1, 0)
        return tt // tiles_per_row, tt % tiles_per_row

    outs = pl.pallas_call(
        functools.partial(_ffn_proj_kernel, tiles_per_row=tiles_per_row),
        out_shape=(jax.ShapeDtypeStruct((n, d), F32),) + _proj_out_shapes(b, s, n_gate),
        grid=(n_tiles + 1,),
        in_specs=[
            pl.BlockSpec((tm, d), lambda t: (cur(t), 0)),
            _const_spec(fgain.shape),
            _const_spec(w_gu.shape),
            _const_spec(w_down.shape),
        ] + [_const_spec(c.shape) for c in consts]
        + [pl.BlockSpec((MLA_ROPE // 2, tm), lambda t: (0, prev_row(t)[1]))] * 2,
        out_specs=(
            pl.BlockSpec((tm, d), lambda t: (cur(t), 0)),
            pl.BlockSpec((1, N_HEADS * HEAD_PAD, tm), lambda t: (prev_row(t)[0], 0, prev_row(t)[1])),
            pl.BlockSpec((1, tm, N_HEADS * HEAD_PAD), lambda t: (prev_row(t)[0], prev_row(t)[1], 0)),
            pl.BlockSpec((1, N_HEADS * V_DIM, tm), lambda t: (prev_row(t)[0], 0, prev_row(t)[1])),
            pl.BlockSpec((1, n_gate, tm), lambda t: (prev_row(t)[0], 0, prev_row(t)[1])),
            pl.BlockSpec((1, N_HEADS, 1, tm), lambda t: (prev_row(t)[0], 0, 0, prev_row(t)[1])),
        ),
        scratch_shapes=[pltpu.VMEM((tm, d), F32), pltpu.VMEM((FOX_HEADS, 128), F32)],
        compiler_params=pltpu.CompilerParams(
            dimension_semantics=("arbitrary",), vmem_limit_bytes=V7X_VMEM_LIMIT_BYTES),
        name="ffn_proj",
    )(x2d, fgain, w_gu, w_down, *consts, cosT, sinT)
    return outs


def _attn_kernel(qT_ref, k_ref, vT_ref, dg_ref, km_ref, vmT_ref, o_ref, *, tq, group):
    s_len = qT_ref.shape[2]
    row = lax.broadcasted_iota(jnp.int32, (tq, tq), 0)
    col = lax.broadcasted_iota(jnp.int32, (tq, tq), 1)
    causal = row <= col
    both = lax.broadcasted_iota(jnp.int32, (tq, 2 * tq), 1)
    row2 = lax.broadcasted_iota(jnp.int32, (tq, 2 * tq), 0)
    causal_pair = (row2 <= both) | (both >= tq)
    ones_real = jnp.ones((16, s_len), BF16)
    ones_meta = jnp.ones((16, N_META), BF16)
    v_ext = [jnp.concatenate([vT_ref[0, g * V_DIM:(g + 1) * V_DIM, :], ones_real], axis=0)
             for g in range(group)]
    vm_ext = [jnp.concatenate([vmT_ref[g * V_DIM:(g + 1) * V_DIM, :], ones_meta], axis=0)
              for g in range(group)]
    q_tail_zeros = jnp.zeros((HEAD_PAD - SHIFT_ROW - 8, 2 * tq), F32)
    q_head_zeros = jnp.zeros((SHIFT_ROW - MLA_QK, 2 * tq), F32)

    def meta_scores(i, g, exact):
        head = slice(g * HEAD_PAD, (g + 1) * HEAD_PAD)
        qT = qT_ref[0, head, 2 * i * tq:(2 * i + 2) * tq]
        sm = jnp.dot(km_ref[:, head], qT, preferred_element_type=F32)
        shift = None
        if not exact:
            own = dg_ref[0, g, :, 2 * i * tq:(2 * i + 2) * tq]
            shift = jnp.maximum(jnp.max(sm, axis=0, keepdims=True), own)
            parts = _rows8([-part for part in _bf16_parts(shift)], 2 * tq)
            tail = jnp.concatenate([q_head_zeros, parts, q_tail_zeros], axis=0).astype(BF16)
            qT = jnp.concatenate([qT[:MLA_QK], tail], axis=0)
        return qT, sm, shift

    def scores(i, g, qT, sm, shift):
        head = slice(g * HEAD_PAD, (g + 1) * HEAD_PAD)
        n1 = (2 * i + 1) * tq
        s_main = jnp.dot(k_ref[0, :n1, head], qT, preferred_element_type=F32)
        s_low = jnp.dot(k_ref[0, n1:n1 + tq, head], qT[:, tq:], preferred_element_type=F32)
        return s_main, s_low, sm, shift

    def finish(i, g, s_main, s_low, sm, shift, exact):
        n1 = (2 * i + 1) * tq
        diag = jnp.where(causal_pair, s_main[n1 - tq:], MASK_VALUE)
        low = jnp.where(causal, s_low, MASK_VALUE)
        top = s_main[:n1 - tq] if i > 0 else None
        if exact:
            m = jnp.maximum(jnp.max(diag, axis=0, keepdims=True), jnp.max(sm, axis=0, keepdims=True))
            if i > 0:
                m = jnp.maximum(m, jnp.max(top, axis=0, keepdims=True))
            m = jnp.concatenate([m[:, :tq], jnp.maximum(m[:, tq:], jnp.max(low, axis=0, keepdims=True))],
                                axis=1)
            diag, low, sm = diag - m, low - m[:, tq:], sm - m
            top = top - m if i > 0 else None
        else:
            sm = sm - shift
        parts = [jnp.exp2(diag).astype(BF16)]
        if i > 0:
            parts = [jnp.exp2(top).astype(BF16)] + parts
        p = jnp.concatenate(parts, axis=0) if len(parts) > 1 else parts[0]
        p_low = jnp.exp2(low).astype(BF16)
        pm = jnp.exp2(sm).astype(BF16)
        o = (jnp.dot(v_ext[g][:, :n1], p, preferred_element_type=F32)
             + jnp.dot(vm_ext[g], pm, preferred_element_type=F32))
        o_hi = o[:, tq:] + jnp.dot(v_ext[g][:, n1:n1 + tq], p_low, preferred_element_type=F32)
        den = jnp.concatenate([o[V_DIM:V_DIM + 1, :tq], o_hi[V_DIM:V_DIM + 1]], axis=1)
        rows = slice(g * V_DIM, (g + 1) * V_DIM)
        o_ref[0, rows, 2 * i * tq:(2 * i + 1) * tq] = (o[:V_DIM, :tq] * (1.0 / den[:, :tq])).astype(BF16)
        o_ref[0, rows, (2 * i + 1) * tq:(2 * i + 2) * tq] = (o_hi[:V_DIM] * (1.0 / den[:, tq:])).astype(BF16)
        usable = (den > 0.0) & (den < jnp.inf)
        return jnp.where(usable, 0.0, 1.0)

    nq = s_len // (2 * tq)
    order = [t for pair in zip(range(nq - 1, -1, -1), range(nq)) for t in pair][:nq]

    def run(exact):
        flags = jnp.zeros((1, 2 * tq), F32)
        meta = {(i, g): meta_scores(i, g, exact) for i in order for g in range(group)}
        pending = []
        for i in order:
            for g in range(group):
                pending.append((i, g) + scores(i, g, *meta[i, g]))
                if len(pending) > ATTN_LOOKAHEAD:
                    flags = jnp.maximum(flags, finish(*pending.pop(0), exact))
        for unit in pending:
            flags = jnp.maximum(flags, finish(*unit, exact))
        return flags

    flags = run(exact=False)

    @pl.when(jnp.max(flags) > 0.0)
    def _():
        run(exact=True)


def _attention(qT, k, vT, dg, k_meta, vT_meta, *, tq, group):
    b, _, s = qT.shape
    return pl.pallas_call(
        functools.partial(_attn_kernel, tq=tq, group=group),
        out_shape=jax.ShapeDtypeStruct((b, N_HEADS * V_DIM, s), BF16),
        grid=(b, N_HEADS // group),
        in_specs=[
            pl.BlockSpec((1, group * HEAD_PAD, s), lambda i, h: (i, h, 0)),
            pl.BlockSpec((1, s, group * HEAD_PAD), lambda i, h: (i, 0, h)),
            pl.BlockSpec((1, group * V_DIM, s), lambda i, h: (i, h, 0)),
            pl.BlockSpec((1, group, 1, s), lambda i, h: (i, h, 0, 0)),
            pl.BlockSpec((N_META, group * HEAD_PAD), lambda i, h: (0, h)),
            pl.BlockSpec((group * V_DIM, N_META), lambda i, h: (h, 0)),
        ],
        out_specs=pl.BlockSpec((1, group * V_DIM, s), lambda i, h: (i, h, 0)),
        compiler_params=pltpu.CompilerParams(
            dimension_semantics=("arbitrary", "arbitrary"), vmem_limit_bytes=V7X_VMEM_LIMIT_BYTES),
        name="attention",
    )(qT, k, vT, dg, k_meta, vT_meta)


def _out_ffn_kernel(oT_ref, gT_ref, h_ref, wbfT_ref, wbmT_ref, woT_ref, gain_ref, wgu_ref, wd_ref, o_ref, *, sub):
    tm = h_ref.shape[1]
    d = woT_ref.shape[0]
    n_sub = tm // sub

    def branches(j):
        lanes = slice(j * sub, (j + 1) * sub)
        yf = jnp.dot(wbfT_ref[...], oT_ref[0, :FOX_W, lanes], preferred_element_type=F32)
        ym = jnp.dot(wbmT_ref[...], oT_ref[0, FOX_W:, lanes], preferred_element_type=F32)
        gf = gT_ref[0, :d, lanes].astype(F32)
        gm = gT_ref[0, d:, lanes].astype(F32)
        return (gf * yf + gm * ym).astype(BF16)

    def mixed(j, z):
        mix = lax.dot_general(z, woT_ref[...], _TN, preferred_element_type=F32)
        return h_ref[0, j * sub:(j + 1) * sub, :] + mix

    zs = [branches(j) for j in range(n_sub)]
    h2 = [mixed(j, zs[j]) for j in range(n_sub)]
    gu = [_gate_up(h2[j], gain_ref[...], wgu_ref) for j in range(n_sub)]
    for j in range(n_sub):
        o_ref[0, j * sub:(j + 1) * sub, :] = _half_ffn_out(h2[j], *gu[j], wd_ref)


def _out_ffn(oT, gT, h3d, wbfT, wbmT, woT, gain, w_gu, w_down, *, tm, sub):
    b, s, d = h3d.shape
    return pl.pallas_call(
        functools.partial(_out_ffn_kernel, sub=sub),
        out_shape=jax.ShapeDtypeStruct((b, s, d), F32),
        grid=(b, s // tm),
        in_specs=[
            pl.BlockSpec((1, oT.shape[1], tm), lambda i, t: (i, 0, t)),
            pl.BlockSpec((1, gT.shape[1], tm), lambda i, t: (i, 0, t)),
            pl.BlockSpec((1, tm, d), lambda i, t: (i, t, 0)),
            _const_spec(wbfT.shape),
            _const_spec(wbmT.shape),
            _const_spec(woT.shape),
            _const_spec(gain.shape),
            _const_spec(w_gu.shape),
            _const_spec(w_down.shape),
        ],
        out_specs=pl.BlockSpec((1, tm, d), lambda i, t: (i, t, 0)),
        compiler_params=pltpu.CompilerParams(
            dimension_semantics=("arbitrary", "arbitrary"), vmem_limit_bytes=V7X_VMEM_LIMIT_BYTES),
        name="out_ffn",
    )(oT, gT, h3d, wbfT, wbmT, woT, gain, w_gu, w_down)


def _col(v):
    return v.astype(F32).reshape(-1, 1)


def _pick_tile(n, pref):
    t = min(n, pref)
    while n % t:
        t //= 2
    return t


def kernel(x, meta_tokens, ffn1_norm, ffn1_w_gu, ffn1_w_down, mix_norm, w_in, b_forget, b_gate, fox_q_norm,
           fox_k_norm, mla_cq_norm, mla_w_uq, mla_ckv_norm, mla_w_ukv, mla_q_norm, mla_k_norm, w_branch_fox,
           w_branch_mla, w_out, ffn2_norm, ffn2_w_gu, ffn2_w_down):
    b, s, d = x.shape
    depth = ffn1_norm.shape[0]
    if depth != 1:
        raise NotImplementedError("only depth 1 is supported")
    meta_rows = 128

    pos = jnp.arange(N_META + s, dtype=F32)
    inv_freq = ROPE_THETA ** (-jnp.arange(0, MLA_ROPE, 2, dtype=F32) / MLA_ROPE)
    ang = pos[:, None] * inv_freq[None, :]
    cosT, sinT = jnp.cos(ang).T, jnp.sin(ang).T
    pad = ((0, 0), (0, meta_rows - N_META))
    cos_meta, sin_meta = jnp.pad(cosT[:, :N_META], pad), jnp.pad(sinT[:, :N_META], pad)
    cos_real, sin_real = cosT[:, N_META:], sinT[:, N_META:]

    tm_fused = _pick_tile(s, 256)
    tm_proj = _pick_tile(s, 512)
    sub_proj = _pick_tile(tm_proj, 256)
    tq = _pick_tile(s, 256)

    w1gu, w1d = ffn1_w_gu[0].astype(BF16), ffn1_w_down[0].astype(BF16)
    w2gu, w2d = ffn2_w_gu[0].astype(BF16), ffn2_w_down[0].astype(BF16)
    g1, g2 = ffn1_norm[0].reshape(1, d).astype(F32), ffn2_norm[0].reshape(1, d).astype(F32)
    p = {
        "mix_norm": mix_norm[0].reshape(1, d).astype(F32),
        "w_lowT": w_in[0][:, OFF_FL:OFF_GATE].T.astype(BF16),
        "w_wideT": jnp.concatenate([w_in[0][:, :OFF_FL], w_in[0][:, OFF_GATE:]], axis=1).T.astype(BF16),
        "b_forget": _col(b_forget[0]),
        "b_gate": _col(b_gate[0]),
        "fox_q_norm": _col(fox_q_norm[0]),
        "fox_k_norm": _col(fox_k_norm[0]),
        "mla_cq_norm": _col(mla_cq_norm[0]),
        "mla_w_uqT": mla_w_uq[0].T.astype(BF16),
        "mla_ckv_norm": _col(mla_ckv_norm[0]),
        "mla_w_ukvT": mla_w_ukv[0].T.astype(BF16),
        "mla_q_norm": _col(mla_q_norm[0]),
        "mla_k_norm": _col(mla_k_norm[0]),
    }
    wbfT = w_branch_fox[0].T.astype(BF16)
    wbmT = w_branch_mla[0].T.astype(BF16)
    woT = w_out[0].astype(BF16)

    hm = jnp.pad(meta_tokens.astype(F32), ((0, meta_rows - N_META), (0, 0)))
    hm1 = _ffn(hm, g1, w1gu, w1d, tm=meta_rows)
    _, km, vmT, _, _ = _proj_meta(hm1[None], cos_meta, sin_meta, p)
    k_meta, vT_meta = km[0, :N_META], vmT[0, :, :N_META]

    h1, qT, k, vT, gT, dg = _ffn_proj(x.astype(F32).reshape(b * s, d), g1, w1gu, w1d, cos_real, sin_real, p,
                                  b=b, s=s, tm=tm_fused)
    h1 = h1.reshape(b, s, d)
    oT = _attention(qT, k, vT, dg, k_meta, vT_meta, tq=tq, group=ATTN_GROUP)
    h3 = _out_ffn(oT, gT, h1, wbfT, wbmT, woT, g2, w2gu, w2d, tm=tm_proj, sub=sub_proj)
    return h3.astype(x.dtype)
```

```python
import collections
import functools

import jax
import jax.numpy as jnp
from jax import lax
from jax.experimental import pallas as pl
from jax.experimental.pallas import tpu as pltpu

F32 = jnp.float32
BF16 = jnp.bfloat16

EPS = 1e-6
N_META = 16
FOX_HEADS = 8
FOX_DIM = 64
FOX_W = FOX_HEADS * FOX_DIM
MLA_HEADS = 8
MLA_Q_RANK = 256
MLA_KV_RANK = 128
MLA_NOPE = 64
MLA_ROPE = 32
MLA_QK = MLA_NOPE + MLA_ROPE
MLA_V = 64
ROPE_THETA = 10000.0
LOG2E = 1.4426950408889634

N_HEADS = FOX_HEADS + MLA_HEADS
HEAD_PAD = 128
V_DIM = 64

OFF_FQ = 0
OFF_FK = OFF_FQ + FOX_W
OFF_FV = OFF_FK + FOX_W
OFF_FL = OFF_FV + FOX_W
OFF_CQ = OFF_FL + FOX_HEADS
OFF_CKV = OFF_CQ + MLA_Q_RANK
OFF_KR = OFF_CKV + MLA_KV_RANK
OFF_GATE = OFF_KR + MLA_ROPE
LOW_FL = 0
LOW_CQ = LOW_FL + FOX_HEADS
LOW_CKV = LOW_CQ + MLA_Q_RANK
LOW_KR = LOW_CKV + MLA_KV_RANK
LOW_ROWS = LOW_KR + MLA_ROPE
WIDE_FQ = 0
WIDE_FK = WIDE_FQ + FOX_W
WIDE_FV = WIDE_FK + FOX_W
WIDE_GATE = WIDE_FV + FOX_W

V7X_VMEM_LIMIT_BYTES = 56 * 1024 * 1024
MASK_VALUE = -1e30
ATTN_GROUP = 4
ATTN_LOOKAHEAD = 2
GATE_SPLIT_EIGHTHS = (0, 2, 4, 7, 8)
SHIFT_ROW = 104

_NT = (((1,), (1,)), ((), ()))
_TN = (((0,), (0,)), ((), ()))


def _const_spec(shape):
    zeros = (0,) * len(shape)
    return pl.BlockSpec(shape, lambda *_: zeros, pipeline_mode=pl.Buffered(1))


def _gate_up(x, gain, wgu_ref):
    d_ff = wgu_ref.shape[1] // 2
    ms = jnp.mean(x * x, axis=-1, keepdims=True)
    u = (x * lax.rsqrt(ms + EPS) * gain).astype(BF16)
    g = jnp.dot(u, wgu_ref[:, :d_ff], preferred_element_type=F32)
    up = jnp.dot(u, wgu_ref[:, d_ff:], preferred_element_type=F32)
    return g, up


def _half_ffn_out(x, g, up, wd_ref):
    a = (g * jax.nn.sigmoid(g) * up).astype(BF16)
    return x + 0.5 * jnp.dot(a, wd_ref[...], preferred_element_type=F32)


def _ffn_kernel(x_ref, gain_ref, wgu_ref, wd_ref, o_ref):
    x = x_ref[...]
    g, up = _gate_up(x, gain_ref[...], wgu_ref)
    o_ref[...] = _half_ffn_out(x, g, up, wd_ref)


def _ffn(x2d, gain, w_gu, w_down, *, tm):
    n, d = x2d.shape
    d_ff = w_down.shape[0]
    return pl.pallas_call(
        _ffn_kernel,
        out_shape=jax.ShapeDtypeStruct((n, d), F32),
        grid=(n // tm,),
        in_specs=[
            pl.BlockSpec((tm, d), lambda i: (i, 0)),
            _const_spec((1, d)),
            _const_spec((d, 2 * d_ff)),
            _const_spec((d_ff, d)),
        ],
        out_specs=pl.BlockSpec((tm, d), lambda i: (i, 0)),
        compiler_params=pltpu.CompilerParams(
            dimension_semantics=("arbitrary",), vmem_limit_bytes=V7X_VMEM_LIMIT_BYTES),
        name="ffn",
    )(x2d, gain, w_gu, w_down)


def _bf16_parts(c):
    hi = c.astype(BF16).astype(F32)
    r = c - hi
    mid = r.astype(BF16).astype(F32)
    lo = (r - mid).astype(BF16).astype(F32)
    return hi, mid, lo


def _rows8(vals, tm):
    row = lax.broadcasted_iota(jnp.int32, (8, tm), 0)
    out = jnp.zeros((8, tm), F32)
    for j, v in enumerate(vals):
        out = jnp.where(row == j, v, out)
    return out


def _key_tail(one_rows, first_row, tm):
    return jnp.concatenate([jnp.zeros((SHIFT_ROW - first_row, tm), F32), one_rows,
                            jnp.zeros((HEAD_PAD - SHIFT_ROW - 8, tm), F32)], axis=0)


def _rms_rows(x, n):
    return lax.rsqrt(jnp.sum(x * x, axis=0, keepdims=True) * (1.0 / n) + EPS)


def _rope_rows(x, cos, sin):
    x1 = x[:MLA_ROPE // 2]
    x2 = x[MLA_ROPE // 2:]
    return x1 * cos - x2 * sin, x1 * sin + x2 * cos


_ProjRefs = collections.namedtuple(
    "_ProjRefs", "gain wlowT wwideT bf bg gq gk gcq wuqT gckv wukvT gmq gmk cos sin qT k vT gT dg")


class _ProjOps:
    def __init__(self, u, r, tm, meta):
        self.u, self.r, self.tm, self.meta = u, r, tm, meta
        n_gate = r.wwideT.shape[0] - WIDE_GATE
        self.bounds = [WIDE_GATE + n_gate * e // 8 for e in GATE_SPLIT_EIGHTHS]

    def _wide_rows(self, r0, r1):
        return lax.dot_general(self.r.wwideT[r0:r1, :], self.u, _NT, preferred_element_type=F32)

    def low(self):
        return lax.dot_general(self.r.wlowT[...], self.u, _NT, preferred_element_type=F32)

    def fox(self):
        return self._wide_rows(0, WIDE_GATE)

    def gate_logits(self, j):
        return self._wide_rows(self.bounds[j], self.bounds[j + 1])

    def store_gates(self, j, logits):
        rows = slice(self.bounds[j] - WIDE_GATE, self.bounds[j + 1] - WIDE_GATE)
        self.r.gT[0, rows, :] = jax.nn.sigmoid(logits + self.r.bg[rows, :]).astype(BF16)

    def forget_cumsum(self, low, carry):
        tm = self.tm
        fl = low[LOW_FL:LOW_FL + FOX_HEADS] + self.r.bf[...]
        lf = jnp.minimum(fl, 0.0) - jnp.log1p(jnp.exp(-jnp.abs(fl)))
        r_i = lax.broadcasted_iota(jnp.int32, (tm, tm), 0)
        c_i = lax.broadcasted_iota(jnp.int32, (tm, tm), 1)
        upper = jnp.where(r_i <= c_i, 1.0, 0.0).astype(BF16)
        cs = None
        for part in _bf16_parts(lf):
            d = jnp.dot(part.astype(BF16), upper, preferred_element_type=F32)
            cs = d if cs is None else cs + d
        if self.meta:
            return (cs - cs[:, N_META - 1:N_META]) * LOG2E, None
        return (cs + carry) * LOG2E, carry + cs[:, tm - 1:tm]

    def mla_up(self, low):
        r = self.r
        cq = low[LOW_CQ:LOW_CQ + MLA_Q_RANK]
        cqn = (cq * _rms_rows(cq, MLA_Q_RANK) * r.gcq[...]).astype(BF16)
        qm = jnp.dot(r.wuqT[...], cqn, preferred_element_type=F32)
        ckv = low[LOW_CKV:LOW_CKV + MLA_KV_RANK]
        ckvn = (ckv * _rms_rows(ckv, MLA_KV_RANK) * r.gckv[...]).astype(BF16)
        kv = jnp.dot(r.wukvT[...], ckvn, preferred_element_type=F32)
        return qm, kv

    def store_mla_heads(self, low, qm, kv):
        r, tm = self.r, self.tm
        kr = low[LOW_KR:LOW_KR + MLA_ROPE]
        ss_kr = jnp.sum(kr * kr, axis=0, keepdims=True)
        cos = r.cos[...]
        sin = r.sin[...]
        gmq = r.gmq[...]
        gmk = r.gmk[...]
        zpad = jnp.zeros((HEAD_PAD - MLA_QK, tm), F32)
        ones = jnp.ones((1, tm), F32)
        ktail = _key_tail(_rows8([ones, ones, ones], tm), MLA_QK, tm)
        for hh in range(MLA_HEADS):
            qh = qm[hh * MLA_QK:(hh + 1) * MLA_QK]
            qn = qh * (_rms_rows(qh, MLA_QK) * (MLA_QK ** -0.5 * LOG2E)) * gmq
            q1, q2 = _rope_rows(qn[MLA_NOPE:], cos, sin)
            q_ext = jnp.concatenate([qn[:MLA_NOPE], q1, q2, zpad], axis=0)
            kn_raw = kv[hh * (MLA_NOPE + MLA_V):hh * (MLA_NOPE + MLA_V) + MLA_NOPE]
            rk = lax.rsqrt((jnp.sum(kn_raw * kn_raw, axis=0, keepdims=True) + ss_kr) * (1.0 / MLA_QK) + EPS)
            kn = kn_raw * rk * gmk[:MLA_NOPE]
            k1, k2 = _rope_rows(kr * rk * gmk[MLA_NOPE:], cos, sin)
            k_ext = jnp.concatenate([kn, k1, k2, ktail], axis=0)
            g = FOX_HEADS + hh
            r.qT[0, g * HEAD_PAD:(g + 1) * HEAD_PAD, :] = q_ext.astype(BF16)
            r.k[0, :, g * HEAD_PAD:(g + 1) * HEAD_PAD] = k_ext.T.astype(BF16)
            r.dg[0, g] = jnp.sum(q_ext * k_ext, axis=0, keepdims=True)
            r.vT[0, g * V_DIM:(g + 1) * V_DIM, :] = (
                kv[hh * (MLA_NOPE + MLA_V) + MLA_NOPE:(hh + 1) * (MLA_NOPE + MLA_V)].astype(BF16))

    def store_fox_heads(self, fox, c):
        r, tm = self.r, self.tm
        ones = jnp.ones((1, tm), F32)
        one_rows = _rows8([ones, ones, ones], tm)
        zpad = jnp.zeros((HEAD_PAD - FOX_DIM - 16, tm), F32)
        ktail = _key_tail(one_rows, FOX_DIM + 16, tm)
        gq = r.gq[...]
        gk = r.gk[...]
        for hh in range(FOX_HEADS):
            qh = fox[WIDE_FQ + hh * FOX_DIM:WIDE_FQ + (hh + 1) * FOX_DIM]
            kh = fox[WIDE_FK + hh * FOX_DIM:WIDE_FK + (hh + 1) * FOX_DIM]
            qn = qh * (_rms_rows(qh, FOX_DIM) * (FOX_DIM ** -0.5 * LOG2E)) * gq
            kn = kh * _rms_rows(kh, FOX_DIM) * gk
            c_rows = _rows8(list(_bf16_parts(c[hh:hh + 1])), tm)
            q_ext = jnp.concatenate([qn, c_rows, one_rows, zpad], axis=0)
            k_ext = jnp.concatenate([kn, one_rows, -c_rows, ktail], axis=0)
            r.qT[0, hh * HEAD_PAD:(hh + 1) * HEAD_PAD, :] = q_ext.astype(BF16)
            r.k[0, :, hh * HEAD_PAD:(hh + 1) * HEAD_PAD] = k_ext.T.astype(BF16)
            r.dg[0, hh] = jnp.sum(q_ext * k_ext, axis=0, keepdims=True)
            r.vT[0, hh * V_DIM:(hh + 1) * V_DIM, :] = (
                fox[WIDE_FV + hh * FOX_DIM:WIDE_FV + (hh + 1) * FOX_DIM].astype(BF16))


def _rms_norm_bf16(x, gain):
    ms = jnp.mean(x * x, axis=-1, keepdims=True)
    return (x * lax.rsqrt(ms + EPS) * gain).astype(BF16)


def _proj_meta_kernel(h_ref, *refs):
    r = _ProjRefs(*refs)
    tm = h_ref.shape[1]
    ops = _ProjOps(_rms_norm_bf16(h_ref[0], r.gain[...]), r, tm, meta=True)
    low = ops.low()
    fox = ops.fox()
    c, _ = ops.forget_cumsum(low, None)
    qm, kv = ops.mla_up(low)
    ops.store_fox_heads(fox, c)
    ops.store_mla_heads(low, qm, kv)
    for j in range(len(ops.bounds) - 1):
        ops.store_gates(j, ops.gate_logits(j))


def _ffn_proj_kernel(x_ref, fgain_ref, wgu_ref, wd_ref, *refs, tiles_per_row):
    *proj_refs, h1_ref, qT_ref, k_ref, vT_ref, gT_ref, dg_ref, hprev_ref, carry_ref = refs
    r = _ProjRefs(*proj_refs, qT_ref, k_ref, vT_ref, gT_ref, dg_ref)
    t = pl.program_id(0)
    tm = x_ref.shape[0]

    @pl.when(t == 0)
    def _():
        hprev_ref[...] = jnp.zeros_like(hprev_ref)
        carry_ref[...] = jnp.zeros_like(carry_ref)

    ops = _ProjOps(_rms_norm_bf16(hprev_ref[...], r.gain[...]), r, tm, meta=False)
    first_of_row = lax.rem(t + (tiles_per_row - 1), tiles_per_row) == 0
    carry = jnp.where(first_of_row, 0.0, carry_ref[:, 0:1])
    x = x_ref[...]

    low = ops.low()
    g, up = _gate_up(x, fgain_ref[...], wgu_ref)
    c, carry = ops.forget_cumsum(low, carry)
    qm, kv = ops.mla_up(low)
    fox = ops.fox()
    ops.store_mla_heads(low, qm, kv)
    h1 = _half_ffn_out(x, g, up, wd_ref)
    ops.store_fox_heads(fox, c)
    n_chunks = len(ops.bounds) - 1
    gate = ops.gate_logits(0)
    for j in range(n_chunks):
        gate_next = ops.gate_logits(j + 1) if j + 1 < n_chunks else None
        ops.store_gates(j, gate)
        gate = gate_next
    h1_ref[...] = h1
    hprev_ref[...] = h1
    carry_ref[...] = jnp.broadcast_to(carry, carry_ref.shape)


def _proj_consts(p):
    return [p["mix_norm"], p["w_lowT"], p["w_wideT"], p["b_forget"], p["b_gate"], p["fox_q_norm"],
            p["fox_k_norm"], p["mla_cq_norm"], p["mla_w_uqT"], p["mla_ckv_norm"], p["mla_w_ukvT"],
            p["mla_q_norm"], p["mla_k_norm"]]


def _proj_out_shapes(b, s, n_gate):
    return (
        jax.ShapeDtypeStruct((b, N_HEADS * HEAD_PAD, s), BF16),
        jax.ShapeDtypeStruct((b, s, N_HEADS * HEAD_PAD), BF16),
        jax.ShapeDtypeStruct((b, N_HEADS * V_DIM, s), BF16),
        jax.ShapeDtypeStruct((b, n_gate, s), BF16),
        jax.ShapeDtypeStruct((b, N_HEADS, 1, s), F32),
    )


def _proj_meta(h3d, cosT, sinT, p):
    _, tm, d = h3d.shape
    n_gate = p["w_wideT"].shape[0] - WIDE_GATE
    consts = _proj_consts(p)
    whole = lambda shape: pl.BlockSpec(shape, lambda i: (0,) * len(shape))
    return pl.pallas_call(
        _proj_meta_kernel,
        out_shape=_proj_out_shapes(1, tm, n_gate),
        grid=(1,),
        in_specs=[whole(h3d.shape)] + [whole(c.shape) for c in consts] + [whole(cosT.shape), whole(sinT.shape)],
        out_specs=tuple(whole(o.shape) for o in _proj_out_shapes(1, tm, n_gate)),
        compiler_params=pltpu.CompilerParams(
            dimension_semantics=("arbitrary",), vmem_limit_bytes=V7X_VMEM_LIMIT_BYTES),
        name="proj_meta",
    )(h3d, *consts, cosT, sinT)


def _ffn_proj(x2d, fgain, w_gu, w_down, cosT, sinT, p, *, b, s, tm):
    n, d = x2d.shape
    n_gate = p["w_wideT"].shape[0] - WIDE_GATE
    consts = _proj_consts(p)
    tiles_per_row = s // tm
    n_tiles = n // tm

    def cur(t):
        return jnp.minimum(t, n_tiles - 1)

    def prev_row(t):
        tt = jnp.maximum(t - 1, 0)
        return tt // tiles_per_row, tt % tiles_per_row

    outs = pl.pallas_call(
        functools.partial(_ffn_proj_kernel, tiles_per_row=tiles_per_row),
        out_shape=(jax.ShapeDtypeStruct((n, d), F32),) + _proj_out_shapes(b, s, n_gate),
        grid=(n_tiles + 1,),
        in_specs=[
            pl.BlockSpec((tm, d), lambda t: (cur(t), 0)),
            _const_spec(fgain.shape),
            _const_spec(w_gu.shape),
            _const_spec(w_down.shape),
        ] + [_const_spec(c.shape) for c in consts]
        + [pl.BlockSpec((MLA_ROPE // 2, tm), lambda t: (0, prev_row(t)[1]))] * 2,
        out_specs=(
            pl.BlockSpec((tm, d), lambda t: (cur(t), 0)),
            pl.BlockSpec((1, N_HEADS * HEAD_PAD, tm), lambda t: (prev_row(t)[0], 0, prev_row(t)[1])),
            pl.BlockSpec((1, tm, N_HEADS * HEAD_PAD), lambda t: (prev_row(t)[0], prev_row(t)[1], 0)),
            pl.BlockSpec((1, N_HEADS * V_DIM, tm), lambda t: (prev_row(t)[0], 0, prev_row(t)[1])),
            pl.BlockSpec((1, n_gate, tm), lambda t: (prev_row(t)[0], 0, prev_row(t)[1])),
            pl.BlockSpec((1, N_HEADS, 1, tm), lambda t: (prev_row(t)[0], 0, 0, prev_row(t)[1])),
        ),
        scratch_shapes=[pltpu.VMEM((tm, d), F32), pltpu.VMEM((FOX_HEADS, 128), F32)],
        compiler_params=pltpu.CompilerParams(
            dimension_semantics=("arbitrary",), vmem_limit_bytes=V7X_VMEM_LIMIT_BYTES),
        name="ffn_proj",
    )(x2d, fgain, w_gu, w_down, *consts, cosT, sinT)
    return outs


def _attn_kernel(qT_ref, k_ref, vT_ref, dg_ref, km_ref, vmT_ref, o_ref, *, tq, group):
    s_len = qT_ref.shape[2]
    row = lax.broadcasted_iota(jnp.int32, (tq, tq), 0)
    col = lax.broadcasted_iota(jnp.int32, (tq, tq), 1)
    causal = row <= col
    both = lax.broadcasted_iota(jnp.int32, (tq, 2 * tq), 1)
    row2 = lax.broadcasted_iota(jnp.int32, (tq, 2 * tq), 0)
    causal_pair = (row2 <= both) | (both >= tq)
    ones_real = jnp.ones((16, s_len), BF16)
    ones_meta = jnp.ones((16, N_META), BF16)
    v_ext = [jnp.concatenate([vT_ref[0, g * V_DIM:(g + 1) * V_DIM, :], ones_real], axis=0)
             for g in range(group)]
    vm_ext = [jnp.concatenate([vmT_ref[g * V_DIM:(g + 1) * V_DIM, :], ones_meta], axis=0)
              for g in range(group)]
    q_tail_zeros = jnp.zeros((HEAD_PAD - SHIFT_ROW - 8, 2 * tq), F32)
    q_head_zeros = jnp.zeros((SHIFT_ROW - MLA_QK, 2 * tq), F32)

    def meta_scores(i, g, exact):
        head = slice(g * HEAD_PAD, (g + 1) * HEAD_PAD)
        qT = qT_ref[0, head, 2 * i * tq:(2 * i + 2) * tq]
        sm = jnp.dot(km_ref[:, head], qT, preferred_element_type=F32)
        shift = None
        if not exact:
            own = dg_ref[0, g, :, 2 * i * tq:(2 * i + 2) * tq]
            shift = jnp.maximum(jnp.max(sm, axis=0, keepdims=True), own)
            parts = _rows8([-part for part in _bf16_parts(shift)], 2 * tq)
            tail = jnp.concatenate([q_head_zeros, parts, q_tail_zeros], axis=0).astype(BF16)
            qT = jnp.concatenate([qT[:MLA_QK], tail], axis=0)
        return qT, sm, shift

    def scores(i, g, qT, sm, shift):
        head = slice(g * HEAD_PAD, (g + 1) * HEAD_PAD)
        n1 = (2 * i + 1) * tq
        s_main = jnp.dot(k_ref[0, :n1, head], qT, preferred_element_type=F32)
        s_low = jnp.dot(k_ref[0, n1:n1 + tq, head], qT[:, tq:], preferred_element_type=F32)
        return s_main, s_low, sm, shift

    def finish(i, g, s_main, s_low, sm, shift, exact):
        n1 = (2 * i + 1) * tq
        diag = jnp.where(causal_pair, s_main[n1 - tq:], MASK_VALUE)
        low = jnp.where(causal, s_low, MASK_VALUE)
        top = s_main[:n1 - tq] if i > 0 else None
        if exact:
            m = jnp.maximum(jnp.max(diag, axis=0, keepdims=True), jnp.max(sm, axis=0, keepdims=True))
            if i > 0:
                m = jnp.maximum(m, jnp.max(top, axis=0, keepdims=True))
            m = jnp.concatenate([m[:, :tq], jnp.maximum(m[:, tq:], jnp.max(low, axis=0, keepdims=True))],
                                axis=1)
            diag, low, sm = diag - m, low - m[:, tq:], sm - m
            top = top - m if i > 0 else None
        else:
            sm = sm - shift
        parts = [jnp.exp2(diag).astype(BF16)]
        if i > 0:
            parts = [jnp.exp2(top).astype(BF16)] + parts
        p = jnp.concatenate(parts, axis=0) if len(parts) > 1 else parts[0]
        p_low = jnp.exp2(low).astype(BF16)
        pm = jnp.exp2(sm).astype(BF16)
        o = (jnp.dot(v_ext[g][:, :n1], p, preferred_element_type=F32)
             + jnp.dot(vm_ext[g], pm, preferred_element_type=F32))
        o_hi = o[:, tq:] + jnp.dot(v_ext[g][:, n1:n1 + tq], p_low, preferred_element_type=F32)
        den = jnp.concatenate([o[V_DIM:V_DIM + 1, :tq], o_hi[V_DIM:V_DIM + 1]], axis=1)
        rows = slice(g * V_DIM, (g + 1) * V_DIM)
        o_ref[0, rows, 2 * i * tq:(2 * i + 1) * tq] = (o[:V_DIM, :tq] * (1.0 / den[:, :tq])).astype(BF16)
        o_ref[0, rows, (2 * i + 1) * tq:(2 * i + 2) * tq] = (o_hi[:V_DIM] * (1.0 / den[:, tq:])).astype(BF16)
        usable = (den > 0.0) & (den < jnp.inf)
        return jnp.where(usable, 0.0, 1.0)

    nq = s_len // (2 * tq)
    order = [t for pair in zip(range(nq - 1, -1, -1), range(nq)) for t in pair][:nq]

    def run(exact):
        flags = jnp.zeros((1, 2 * tq), F32)
        meta = {(i, g): meta_scores(i, g, exact) for i in order for g in range(group)}
        pending = []
        for i in order:
            for g in range(group):
                pending.append((i, g) + scores(i, g, *meta[i, g]))
                if len(pending) > ATTN_LOOKAHEAD:
                    flags = jnp.maximum(flags, finish(*pending.pop(0), exact))
        for unit in pending:
            flags = jnp.maximum(flags, finish(*unit, exact))
        return flags

    flags = run(exact=False)

    @pl.when(jnp.max(flags) > 0.0)
    def _():
        run(exact=True)


def _attention(qT, k, vT, dg, k_meta, vT_meta, *, tq, group):
    b, _, s = qT.shape
    return pl.pallas_call(
        functools.partial(_attn_kernel, tq=tq, group=group),
        out_shape=jax.ShapeDtypeStruct((b, N_HEADS * V_DIM, s), BF16),
        grid=(b, N_HEADS // group),
        in_specs=[
            pl.BlockSpec((1, group * HEAD_PAD, s), lambda i, h: (i, h, 0)),
            pl.BlockSpec((1, s, group * HEAD_PAD), lambda i, h: (i, 0, h)),
            pl.BlockSpec((1, group * V_DIM, s), lambda i, h: (i, h, 0)),
            pl.BlockSpec((1, group, 1, s), lambda i, h: (i, h, 0, 0)),
            pl.BlockSpec((N_META, group * HEAD_PAD), lambda i, h: (0, h)),
            pl.BlockSpec((group * V_DIM, N_META), lambda i, h: (h, 0)),
        ],
        out_specs=pl.BlockSpec((1, group * V_DIM, s), lambda i, h: (i, h, 0)),
        compiler_params=pltpu.CompilerParams(
            dimension_semantics=("arbitrary", "arbitrary"), vmem_limit_bytes=V7X_VMEM_LIMIT_BYTES),
        name="attention",
    )(qT, k, vT, dg, k_meta, vT_meta)


def _out_ffn_kernel(oT_ref, gT_ref, h_ref, wbfT_ref, wbmT_ref, woT_ref, gain_ref, wgu_ref, wd_ref, o_ref, *, sub):
    tm = h_ref.shape[1]
    d = woT_ref.shape[0]
    n_sub = tm // sub

    def branches(j):
        lanes = slice(j * sub, (j + 1) * sub)
        yf = jnp.dot(wbfT_ref[...], oT_ref[0, :FOX_W, lanes], preferred_element_type=F32)
        ym = jnp.dot(wbmT_ref[...], oT_ref[0, FOX_W:, lanes], preferred_element_type=F32)
        gf = gT_ref[0, :d, lanes].astype(F32)
        gm = gT_ref[0, d:, lanes].astype(F32)
        return (gf * yf + gm * ym).astype(BF16)

    def mixed(j, z):
        mix = lax.dot_general(z, woT_ref[...], _TN, preferred_element_type=F32)
        return h_ref[0, j * sub:(j + 1) * sub, :] + mix

    zs = [branches(j) for j in range(n_sub)]
    h2 = [mixed(j, zs[j]) for j in range(n_sub)]
    gu = [_gate_up(h2[j], gain_ref[...], wgu_ref) for j in range(n_sub)]
    for j in range(n_sub):
        o_ref[0, j * sub:(j + 1) * sub, :] = _half_ffn_out(h2[j], *gu[j], wd_ref)


def _out_ffn(oT, gT, h3d, wbfT, wbmT, woT, gain, w_gu, w_down, *, tm, sub):
    b, s, d = h3d.shape
    return pl.pallas_call(
        functools.partial(_out_ffn_kernel, sub=sub),
        out_shape=jax.ShapeDtypeStruct((b, s, d), F32),
        grid=(b, s // tm),
        in_specs=[
            pl.BlockSpec((1, oT.shape[1], tm), lambda i, t: (i, 0, t)),
            pl.BlockSpec((1, gT.shape[1], tm), lambda i, t: (i, 0, t)),
            pl.BlockSpec((1, tm, d), lambda i, t: (i, t, 0)),
            _const_spec(wbfT.shape),
            _const_spec(wbmT.shape),
            _const_spec(woT.shape),
            _const_spec(gain.shape),
            _const_spec(w_gu.shape),
            _const_spec(w_down.shape),
        ],
        out_specs=pl.BlockSpec((1, tm, d), lambda i, t: (i, t, 0)),
        compiler_params=pltpu.CompilerParams(
            dimension_semantics=("arbitrary", "arbitrary"), vmem_limit_bytes=V7X_VMEM_LIMIT_BYTES),
        name="out_ffn",
    )(oT, gT, h3d, wbfT, wbmT, woT, gain, w_gu, w_down)


def _col(v):
    return v.astype(F32).reshape(-1, 1)


def _pick_tile(n, pref):
    t = min(n, pref)
    while n % t:
        t //= 2
    return t


def kernel(x, meta_tokens, ffn1_norm, ffn1_w_gu, ffn1_w_down, mix_norm, w_in, b_forget, b_gate, fox_q_norm,
           fox_k_norm, mla_cq_norm, mla_w_uq, mla_ckv_norm, mla_w_ukv, mla_q_norm, mla_k_norm, w_branch_fox,
           w_branch_mla, w_out, ffn2_norm, ffn2_w_gu, ffn2_w_down):
    b, s, d = x.shape
    depth = ffn1_norm.shape[0]
    if depth != 1:
        raise NotImplementedError("only depth 1 is supported")
    meta_rows = 128

    pos = jnp.arange(N_META + s, dtype=F32)
    inv_freq = ROPE_THETA ** (-jnp.arange(0, MLA_ROPE, 2, dtype=F32) / MLA_ROPE)
    ang = pos[:, None] * inv_freq[None, :]
    cosT, sinT = jnp.cos(ang).T, jnp.sin(ang).T
    pad = ((0, 0), (0, meta_rows - N_META))
    cos_meta, sin_meta = jnp.pad(cosT[:, :N_META], pad), jnp.pad(sinT[:, :N_META], pad)
    cos_real, sin_real = cosT[:, N_META:], sinT[:, N_META:]

    tm_fused = _pick_tile(s, 256)
    tm_out = _pick_tile(s, 512)
    sub_out = _pick_tile(tm_out, 256)
    tq = _pick_tile(s, 256)

    w1gu, w1d = ffn1_w_gu[0].astype(BF16), ffn1_w_down[0].astype(BF16)
    w2gu, w2d = ffn2_w_gu[0].astype(BF16), ffn2_w_down[0].astype(BF16)
    g1, g2 = ffn1_norm[0].reshape(1, d).astype(F32), ffn2_norm[0].reshape(1, d).astype(F32)
    p = {
        "mix_norm": mix_norm[0].reshape(1, d).astype(F32),
        "w_lowT": w_in[0][:, OFF_FL:OFF_GATE].T.astype(BF16),
        "w_wideT": jnp.concatenate([w_in[0][:, :OFF_FL], w_in[0][:, OFF_GATE:]], axis=1).T.astype(BF16),
        "b_forget": _col(b_forget[0]),
        "b_gate": _col(b_gate[0]),
        "fox_q_norm": _col(fox_q_norm[0]),
        "fox_k_norm": _col(fox_k_norm[0]),
        "mla_cq_norm": _col(mla_cq_norm[0]),
        "mla_w_uqT": mla_w_uq[0].T.astype(BF16),
        "mla_ckv_norm": _col(mla_ckv_norm[0]),
        "mla_w_ukvT": mla_w_ukv[0].T.astype(BF16),
        "mla_q_norm": _col(mla_q_norm[0]),
        "mla_k_norm": _col(mla_k_norm[0]),
    }
    wbfT = w_branch_fox[0].T.astype(BF16)
    wbmT = w_branch_mla[0].T.astype(BF16)
    woT = w_out[0].astype(BF16)

    hm = jnp.pad(meta_tokens.astype(F32), ((0, meta_rows - N_META), (0, 0)))
    hm1 = _ffn(hm, g1, w1gu, w1d, tm=meta_rows)
    _, km, vmT, _, _ = _proj_meta(hm1[None], cos_meta, sin_meta, p)
    k_meta, vT_meta = km[0, :N_META], vmT[0, :, :N_META]

    h1, qT, k, vT, gT, dg = _ffn_proj(x.astype(F32).reshape(b * s, d), g1, w1gu, w1d, cos_real, sin_real, p,
                                  b=b, s=s, tm=tm_fused)
    h1 = h1.reshape(b, s, d)
    oT = _attention(qT, k, vT, dg, k_meta, vT_meta, tq=tq, group=ATTN_GROUP)
    h3 = _out_ffn(oT, gT, h1, wbfT, wbmT, woT, g2, w2gu, w2d, tm=tm_out, sub=sub_out)
    return h3.astype(x.dtype)
```

```python
import collections
import functools

import jax
import jax.numpy as jnp
from jax import lax
from jax.experimental import pallas as pl
from jax.experimental.pallas import tpu as pltpu

F32 = jnp.float32
BF16 = jnp.bfloat16

EPS = 1e-6
N_META = 16
FOX_HEADS = 8
FOX_DIM = 64
FOX_W = FOX_HEADS * FOX_DIM
MLA_HEADS = 8
MLA_Q_RANK = 256
MLA_KV_RANK = 128
MLA_NOPE = 64
MLA_ROPE = 32
MLA_QK = MLA_NOPE + MLA_ROPE
MLA_V = 64
ROPE_THETA = 10000.0
LOG2E = 1.4426950408889634

N_HEADS = FOX_HEADS + MLA_HEADS
HEAD_PAD = 128
V_DIM = 64

OFF_FQ = 0
OFF_FK = OFF_FQ + FOX_W
OFF_FV = OFF_FK + FOX_W
OFF_FL = OFF_FV + FOX_W
OFF_CQ = OFF_FL + FOX_HEADS
OFF_CKV = OFF_CQ + MLA_Q_RANK
OFF_KR = OFF_CKV + MLA_KV_RANK
OFF_GATE = OFF_KR + MLA_ROPE
LOW_FL = 0
LOW_CQ = LOW_FL + FOX_HEADS
LOW_CKV = LOW_CQ + MLA_Q_RANK
LOW_KR = LOW_CKV + MLA_KV_RANK
LOW_ROWS = LOW_KR + MLA_ROPE
WIDE_FQ = 0
WIDE_FK = WIDE_FQ + FOX_W
WIDE_FV = WIDE_FK + FOX_W
WIDE_GATE = WIDE_FV + FOX_W

V7X_VMEM_LIMIT_BYTES = 56 * 1024 * 1024
MASK_VALUE = -1e30
ATTN_GROUP = 4
ATTN_LOOKAHEAD = 2
GATE_SPLIT_EIGHTHS = (0, 2, 4, 7, 8)
SHIFT_ROW = 104

_NT = (((1,), (1,)), ((), ()))
_TN = (((0,), (0,)), ((), ()))


def _const_spec(shape):
    zeros = (0,) * len(shape)
    return pl.BlockSpec(shape, lambda *_: zeros, pipeline_mode=pl.Buffered(1))


def _gate_up(x, gain, wgu_ref):
    d_ff = wgu_ref.shape[1] // 2
    ms = jnp.mean(x * x, axis=-1, keepdims=True)
    u = (x * lax.rsqrt(ms + EPS) * gain).astype(BF16)
    g = jnp.dot(u, wgu_ref[:, :d_ff], preferred_element_type=F32)
    up = jnp.dot(u, wgu_ref[:, d_ff:], preferred_element_type=F32)
    return g, up


def _half_ffn_out(x, g, up, wd_ref):
    a = (g * jax.nn.sigmoid(g) * up).astype(BF16)
    return x + 0.5 * jnp.dot(a, wd_ref[...], preferred_element_type=F32)


def _ffn_kernel(x_ref, gain_ref, wgu_ref, wd_ref, o_ref):
    x = x_ref[...]
    g, up = _gate_up(x, gain_ref[...], wgu_ref)
    o_ref[...] = _half_ffn_out(x, g, up, wd_ref)


def _ffn(x2d, gain, w_gu, w_down, *, tm):
    n, d = x2d.shape
    d_ff = w_down.shape[0]
    return pl.pallas_call(
        _ffn_kernel,
        out_shape=jax.ShapeDtypeStruct((n, d), F32),
        grid=(n // tm,),
        in_specs=[
            pl.BlockSpec((tm, d), lambda i: (i, 0)),
            _const_spec((1, d)),
            _const_spec((d, 2 * d_ff)),
            _const_spec((d_ff, d)),
        ],
        out_specs=pl.BlockSpec((tm, d), lambda i: (i, 0)),
        compiler_params=pltpu.CompilerParams(
            dimension_semantics=("arbitrary",), vmem_limit_bytes=V7X_VMEM_LIMIT_BYTES),
        name="ffn",
    )(x2d, gain, w_gu, w_down)


def _bf16_parts(c):
    hi = c.astype(BF16).astype(F32)
    r = c - hi
    mid = r.astype(BF16).astype(F32)
    lo = (r - mid).astype(BF16).astype(F32)
    return hi, mid, lo


def _rows8(vals, tm):
    row = lax.broadcasted_iota(jnp.int32, (8, tm), 0)
    out = jnp.zeros((8, tm), F32)
    for j, v in enumerate(vals):
        out = jnp.where(row == j, v, out)
    return out


def _key_tail(one_rows, first_row, tm):
    return jnp.concatenate([jnp.zeros((SHIFT_ROW - first_row, tm), F32), one_rows,
                            jnp.zeros((HEAD_PAD - SHIFT_ROW - 8, tm), F32)], axis=0)


def _rms_rows(x, n):
    return lax.rsqrt(jnp.sum(x * x, axis=0, keepdims=True) * (1.0 / n) + EPS)


def _rope_rows(x, cos, sin):
    x1 = x[:MLA_ROPE // 2]
    x2 = x[MLA_ROPE // 2:]
    return x1 * cos - x2 * sin, x1 * sin + x2 * cos


_ProjRefs = collections.namedtuple(
    "_ProjRefs", "gain wlowT wwideT bf bg gq gk gcq wuqT gckv wukvT gmq gmk cos sin qT k vT gT dg")


class _ProjOps:
    def __init__(self, u, r, tm, meta):
        self.u, self.r, self.tm, self.meta = u, r, tm, meta
        n_gate = r.wwideT.shape[0] - WIDE_GATE
        self.bounds = [WIDE_GATE + n_gate * e // 8 for e in GATE_SPLIT_EIGHTHS]

    def _wide_rows(self, r0, r1):
        return lax.dot_general(self.r.wwideT[r0:r1, :], self.u, _NT, preferred_element_type=F32)

    def low(self):
        return lax.dot_general(self.r.wlowT[...], self.u, _NT, preferred_element_type=F32)

    def fox(self):
        return self._wide_rows(0, WIDE_GATE)

    def gate_logits(self, j):
        return self._wide_rows(self.bounds[j], self.bounds[j + 1])

    def store_gates(self, j, logits):
        rows = slice(self.bounds[j] - WIDE_GATE, self.bounds[j + 1] - WIDE_GATE)
        self.r.gT[0, rows, :] = jax.nn.sigmoid(logits + self.r.bg[rows, :]).astype(BF16)

    def forget_cumsum(self, low, carry):
        tm = self.tm
        fl = low[LOW_FL:LOW_FL + FOX_HEADS] + self.r.bf[...]
        lf = jnp.minimum(fl, 0.0) - jnp.log1p(jnp.exp(-jnp.abs(fl)))
        r_i = lax.broadcasted_iota(jnp.int32, (tm, tm), 0)
        c_i = lax.broadcasted_iota(jnp.int32, (tm, tm), 1)
        upper = jnp.where(r_i <= c_i, 1.0, 0.0).astype(BF16)
        cs = None
        for part in _bf16_parts(lf):
            d = jnp.dot(part.astype(BF16), upper, preferred_element_type=F32)
            cs = d if cs is None else cs + d
        if self.meta:
            return (cs - cs[:, N_META - 1:N_META]) * LOG2E, None
        return (cs + carry) * LOG2E, carry + cs[:, tm - 1:tm]

    def mla_up(self, low):
        r = self.r
        cq = low[LOW_CQ:LOW_CQ + MLA_Q_RANK]
        cqn = (cq * _rms_rows(cq, MLA_Q_RANK) * r.gcq[...]).astype(BF16)
        qm = jnp.dot(r.wuqT[...], cqn, preferred_element_type=F32)
        ckv = low[LOW_CKV:LOW_CKV + MLA_KV_RANK]
        ckvn = (ckv * _rms_rows(ckv, MLA_KV_RANK) * r.gckv[...]).astype(BF16)
        kv = jnp.dot(r.wukvT[...], ckvn, preferred_element_type=F32)
        return qm, kv

    def store_mla_heads(self, low, qm, kv):
        r, tm = self.r, self.tm
        kr = low[LOW_KR:LOW_KR + MLA_ROPE]
        ss_kr = jnp.sum(kr * kr, axis=0, keepdims=True)
        cos = r.cos[...]
        sin = r.sin[...]
        gmq = r.gmq[...]
        gmk = r.gmk[...]
        zpad = jnp.zeros((HEAD_PAD - MLA_QK, tm), F32)
        ones = jnp.ones((1, tm), F32)
        ktail = _key_tail(_rows8([ones, ones, ones], tm), MLA_QK, tm)
        for hh in range(MLA_HEADS):
            qh = qm[hh * MLA_QK:(hh + 1) * MLA_QK]
            qn = qh * (_rms_rows(qh, MLA_QK) * (MLA_QK ** -0.5 * LOG2E)) * gmq
            q1, q2 = _rope_rows(qn[MLA_NOPE:], cos, sin)
            q_ext = jnp.concatenate([qn[:MLA_NOPE], q1, q2, zpad], axis=0)
            kn_raw = kv[hh * (MLA_NOPE + MLA_V):hh * (MLA_NOPE + MLA_V) + MLA_NOPE]
            rk = lax.rsqrt((jnp.sum(kn_raw * kn_raw, axis=0, keepdims=True) + ss_kr) * (1.0 / MLA_QK) + EPS)
            kn = kn_raw * rk * gmk[:MLA_NOPE]
            k1, k2 = _rope_rows(kr * rk * gmk[MLA_NOPE:], cos, sin)
            k_ext = jnp.concatenate([kn, k1, k2, ktail], axis=0)
            g = FOX_HEADS + hh
            r.qT[0, g * HEAD_PAD:(g + 1) * HEAD_PAD, :] = q_ext.astype(BF16)
            r.k[0, :, g * HEAD_PAD:(g + 1) * HEAD_PAD] = k_ext.T.astype(BF16)
            r.dg[0, g] = jnp.sum(q_ext * k_ext, axis=0, keepdims=True)
            r.vT[0, g * V_DIM:(g + 1) * V_DIM, :] = (
                kv[hh * (MLA_NOPE + MLA_V) + MLA_NOPE:(hh + 1) * (MLA_NOPE + MLA_V)].astype(BF16))

    def store_fox_heads(self, fox, c):
        r, tm = self.r, self.tm
        ones = jnp.ones((1, tm), F32)
        one_rows = _rows8([ones, ones, ones], tm)
        zpad = jnp.zeros((HEAD_PAD - FOX_DIM - 16, tm), F32)
        ktail = _key_tail(one_rows, FOX_DIM + 16, tm)
        gq = r.gq[...]
        gk = r.gk[...]
        for hh in range(FOX_HEADS):
            qh = fox[WIDE_FQ + hh * FOX_DIM:WIDE_FQ + (hh + 1) * FOX_DIM]
            kh = fox[WIDE_FK + hh * FOX_DIM:WIDE_FK + (hh + 1) * FOX_DIM]
            qn = qh * (_rms_rows(qh, FOX_DIM) * (FOX_DIM ** -0.5 * LOG2E)) * gq
            kn = kh * _rms_rows(kh, FOX_DIM) * gk
            c_rows = _rows8(list(_bf16_parts(c[hh:hh + 1])), tm)
            q_ext = jnp.concatenate([qn, c_rows, one_rows, zpad], axis=0)
            k_ext = jnp.concatenate([kn, one_rows, -c_rows, ktail], axis=0)
            r.qT[0, hh * HEAD_PAD:(hh + 1) * HEAD_PAD, :] = q_ext.astype(BF16)
            r.k[0, :, hh * HEAD_PAD:(hh + 1) * HEAD_PAD] = k_ext.T.astype(BF16)
            r.dg[0, hh] = jnp.sum(q_ext * k_ext, axis=0, keepdims=True)
            r.vT[0, hh * V_DIM:(hh + 1) * V_DIM, :] = (
                fox[WIDE_FV + hh * FOX_DIM:WIDE_FV + (hh + 1) * FOX_DIM].astype(BF16))


def _rms_norm_bf16(x, gain):
    ms = jnp.mean(x * x, axis=-1, keepdims=True)
    return (x * lax.rsqrt(ms + EPS) * gain).astype(BF16)


def _proj_meta_kernel(h_ref, *refs):
    r = _ProjRefs(*refs)
    tm = h_ref.shape[1]
    ops = _ProjOps(_rms_norm_bf16(h_ref[0], r.gain[...]), r, tm, meta=True)
    low = ops.low()
    fox = ops.fox()
    c, _ = ops.forget_cumsum(low, None)
    qm, kv = ops.mla_up(low)
    ops.store_fox_heads(fox, c)
    ops.store_mla_heads(low, qm, kv)
    for j in range(len(ops.bounds) - 1):
        ops.store_gates(j, ops.gate_logits(j))


def _ffn_proj_kernel(x_ref, fgain_ref, wgu_ref, wd_ref, *refs, tiles_per_row):
    *proj_refs, h1_ref, qT_ref, k_ref, vT_ref, gT_ref, dg_ref, hprev_ref, carry_ref = refs
    r = _ProjRefs(*proj_refs, qT_ref, k_ref, vT_ref, gT_ref, dg_ref)
    t = pl.program_id(0)
    tm = x_ref.shape[0]

    @pl.when(t == 0)
    def _():
        hprev_ref[...] = jnp.zeros_like(hprev_ref)
        carry_ref[...] = jnp.zeros_like(carry_ref)

    ops = _ProjOps(_rms_norm_bf16(hprev_ref[...], r.gain[...]), r, tm, meta=False)
    first_of_row = lax.rem(t + (tiles_per_row - 1), tiles_per_row) == 0
    carry = jnp.where(first_of_row, 0.0, carry_ref[:, 0:1])
    x = x_ref[...]

    low = ops.low()
    fox = ops.fox()
    g, up = _gate_up(x, fgain_ref[...], wgu_ref)
    c, carry = ops.forget_cumsum(low, carry)
    qm, kv = ops.mla_up(low)
    ops.store_fox_heads(fox, c)
    gate = ops.gate_logits(0)
    h1 = _half_ffn_out(x, g, up, wd_ref)
    ops.store_mla_heads(low, qm, kv)
    n_chunks = len(ops.bounds) - 1
    for j in range(n_chunks):
        gate_next = ops.gate_logits(j + 1) if j + 1 < n_chunks else None
        ops.store_gates(j, gate)
        gate = gate_next
    h1_ref[...] = h1
    hprev_ref[...] = h1
    carry_ref[...] = jnp.broadcast_to(carry, carry_ref.shape)


def _proj_consts(p):
    return [p["mix_norm"], p["w_lowT"], p["w_wideT"], p["b_forget"], p["b_gate"], p["fox_q_norm"],
            p["fox_k_norm"], p["mla_cq_norm"], p["mla_w_uqT"], p["mla_ckv_norm"], p["mla_w_ukvT"],
            p["mla_q_norm"], p["mla_k_norm"]]


def _proj_out_shapes(b, s, n_gate):
    return (
        jax.ShapeDtypeStruct((b, N_HEADS * HEAD_PAD, s), BF16),
        jax.ShapeDtypeStruct((b, s, N_HEADS * HEAD_PAD), BF16),
        jax.ShapeDtypeStruct((b, N_HEADS * V_DIM, s), BF16),
        jax.ShapeDtypeStruct((b, n_gate, s), BF16),
        jax.ShapeDtypeStruct((b, N_HEADS, 1, s), F32),
    )


def _proj_meta(h3d, cosT, sinT, p):
    _, tm, d = h3d.shape
    n_gate = p["w_wideT"].shape[0] - WIDE_GATE
    consts = _proj_consts(p)
    whole = lambda shape: pl.BlockSpec(shape, lambda i: (0,) * len(shape))
    return pl.pallas_call(
        _proj_meta_kernel,
        out_shape=_proj_out_shapes(1, tm, n_gate),
        grid=(1,),
        in_specs=[whole(h3d.shape)] + [whole(c.shape) for c in consts] + [whole(cosT.shape), whole(sinT.shape)],
        out_specs=tuple(whole(o.shape) for o in _proj_out_shapes(1, tm, n_gate)),
        compiler_params=pltpu.CompilerParams(
            dimension_semantics=("arbitrary",), vmem_limit_bytes=V7X_VMEM_LIMIT_BYTES),
        name="proj_meta",
    )(h3d, *consts, cosT, sinT)


def _ffn_proj(x2d, fgain, w_gu, w_down, cosT, sinT, p, *, b, s, tm):
    n, d = x2d.shape
    n_gate = p["w_wideT"].shape[0] - WIDE_GATE
    consts = _proj_consts(p)
    tiles_per_row = s // tm
    n_tiles = n // tm

    def cur(t):
        return jnp.minimum(t, n_tiles - 1)

    def prev_row(t):
        tt = jnp.maximum(t - 1, 0)
        return tt // tiles_per_row, tt % tiles_per_row

    outs = pl.pallas_call(
        functools.partial(_ffn_proj_kernel, tiles_per_row=tiles_per_row),
        out_shape=(jax.ShapeDtypeStruct((n, d), F32),) + _proj_out_shapes(b, s, n_gate),
        grid=(n_tiles + 1,),
        in_specs=[
            pl.BlockSpec((tm, d), lambda t: (cur(t), 0)),
            _const_spec(fgain.shape),
            _const_spec(w_gu.shape),
            _const_spec(w_down.shape),
        ] + [_const_spec(c.shape) for c in consts]
        + [pl.BlockSpec((MLA_ROPE // 2, tm), lambda t: (0, prev_row(t)[1]))] * 2,
        out_specs=(
            pl.BlockSpec((tm, d), lambda t: (cur(t), 0)),
            pl.BlockSpec((1, N_HEADS * HEAD_PAD, tm), lambda t: (prev_row(t)[0], 0, prev_row(t)[1])),
            pl.BlockSpec((1, tm, N_HEADS * HEAD_PAD), lambda t: (prev_row(t)[0], prev_row(t)[1], 0)),
            pl.BlockSpec((1, N_HEADS * V_DIM, tm), lambda t: (prev_row(t)[0], 0, prev_row(t)[1])),
            pl.BlockSpec((1, n_gate, tm), lambda t: (prev_row(t)[0], 0, prev_row(t)[1])),
            pl.BlockSpec((1, N_HEADS, 1, tm), lambda t: (prev_row(t)[0], 0, 0, prev_row(t)[1])),
        ),
        scratch_shapes=[pltpu.VMEM((tm, d), F32), pltpu.VMEM((FOX_HEADS, 128), F32)],
        compiler_params=pltpu.CompilerParams(
            dimension_semantics=("arbitrary",), vmem_limit_bytes=V7X_VMEM_LIMIT_BYTES),
        name="ffn_proj",
    )(x2d, fgain, w_gu, w_down, *consts, cosT, sinT)
    return outs


def _attn_kernel(qT_ref, k_ref, vT_ref, dg_ref, km_ref, vmT_ref, o_ref, *, tq, group):
    s_len = qT_ref.shape[2]
    row = lax.broadcasted_iota(jnp.int32, (tq, tq), 0)
    col = lax.broadcasted_iota(jnp.int32, (tq, tq), 1)
    causal = row <= col
    both = lax.broadcasted_iota(jnp.int32, (tq, 2 * tq), 1)
    row2 = lax.broadcasted_iota(jnp.int32, (tq, 2 * tq), 0)
    causal_pair = (row2 <= both) | (both >= tq)
    ones_real = jnp.ones((16, s_len), BF16)
    ones_meta = jnp.ones((16, N_META), BF16)
    v_ext = [jnp.concatenate([vT_ref[0, g * V_DIM:(g + 1) * V_DIM, :], ones_real], axis=0)
             for g in range(group)]
    vm_ext = [jnp.concatenate([vmT_ref[g * V_DIM:(g + 1) * V_DIM, :], ones_meta], axis=0)
              for g in range(group)]
    q_tail_zeros = jnp.zeros((HEAD_PAD - SHIFT_ROW - 8, 2 * tq), F32)
    q_head_zeros = jnp.zeros((SHIFT_ROW - MLA_QK, 2 * tq), F32)

    def meta_scores(i, g, exact):
        head = slice(g * HEAD_PAD, (g + 1) * HEAD_PAD)
        qT = qT_ref[0, head, 2 * i * tq:(2 * i + 2) * tq]
        sm = jnp.dot(km_ref[:, head], qT, preferred_element_type=F32)
        shift = None
        if not exact:
            own = dg_ref[0, g, :, 2 * i * tq:(2 * i + 2) * tq]
            shift = jnp.maximum(jnp.max(sm, axis=0, keepdims=True), own)
            parts = _rows8([-part for part in _bf16_parts(shift)], 2 * tq)
            tail = jnp.concatenate([q_head_zeros, parts, q_tail_zeros], axis=0).astype(BF16)
            qT = jnp.concatenate([qT[:MLA_QK], tail], axis=0)
        return qT, sm, shift

    def scores(i, g, qT, sm, shift):
        head = slice(g * HEAD_PAD, (g + 1) * HEAD_PAD)
        n1 = (2 * i + 1) * tq
        s_main = jnp.dot(k_ref[0, :n1, head], qT, preferred_element_type=F32)
        s_low = jnp.dot(k_ref[0, n1:n1 + tq, head], qT[:, tq:], preferred_element_type=F32)
        return s_main, s_low, sm, shift

    def finish(i, g, s_main, s_low, sm, shift, exact):
        n1 = (2 * i + 1) * tq
        diag = jnp.where(causal_pair, s_main[n1 - tq:], MASK_VALUE)
        low = jnp.where(causal, s_low, MASK_VALUE)
        top = s_main[:n1 - tq] if i > 0 else None
        if exact:
            m = jnp.maximum(jnp.max(diag, axis=0, keepdims=True), jnp.max(sm, axis=0, keepdims=True))
            if i > 0:
                m = jnp.maximum(m, jnp.max(top, axis=0, keepdims=True))
            m = jnp.concatenate([m[:, :tq], jnp.maximum(m[:, tq:], jnp.max(low, axis=0, keepdims=True))],
                                axis=1)
            diag, low, sm = diag - m, low - m[:, tq:], sm - m
            top = top - m if i > 0 else None
        else:
            sm = sm - shift
        parts = [jnp.exp2(diag).astype(BF16)]
        if i > 0:
            parts = [jnp.exp2(top).astype(BF16)] + parts
        p = jnp.concatenate(parts, axis=0) if len(parts) > 1 else parts[0]
        p_low = jnp.exp2(low).astype(BF16)
        pm = jnp.exp2(sm).astype(BF16)
        o = (jnp.dot(v_ext[g][:, :n1], p, preferred_element_type=F32)
             + jnp.dot(vm_ext[g], pm, preferred_element_type=F32))
        o_hi = o[:, tq:] + jnp.dot(v_ext[g][:, n1:n1 + tq], p_low, preferred_element_type=F32)
        den = jnp.concatenate([o[V_DIM:V_DIM + 1, :tq], o_hi[V_DIM:V_DIM + 1]], axis=1)
        rows = slice(g * V_DIM, (g + 1) * V_DIM)
        o_ref[0, rows, 2 * i * tq:(2 * i + 1) * tq] = (o[:V_DIM, :tq] * (1.0 / den[:, :tq])).astype(BF16)
        o_ref[0, rows, (2 * i + 1) * tq:(2 * i + 2) * tq] = (o_hi[:V_DIM] * (1.0 / den[:, tq:])).astype(BF16)
        usable = (den > 0.0) & (den < jnp.inf)
        return jnp.where(usable, 0.0, 1.0)

    nq = s_len // (2 * tq)
    order = [t for pair in zip(range(nq - 1, -1, -1), range(nq)) for t in pair][:nq]

    def run(exact):
        flags = jnp.zeros((1, 2 * tq), F32)
        meta = {(i, g): meta_scores(i, g, exact) for i in order for g in range(group)}
        pending = []
        for i in order:
            for g in range(group):
                pending.append((i, g) + scores(i, g, *meta[i, g]))
                if len(pending) > ATTN_LOOKAHEAD:
                    flags = jnp.maximum(flags, finish(*pending.pop(0), exact))
        for unit in pending:
            flags = jnp.maximum(flags, finish(*unit, exact))
        return flags

    flags = run(exact=False)

    @pl.when(jnp.max(flags) > 0.0)
    def _():
        run(exact=True)


def _attention(qT, k, vT, dg, k_meta, vT_meta, *, tq, group):
    b, _, s = qT.shape
    return pl.pallas_call(
        functools.partial(_attn_kernel, tq=tq, group=group),
        out_shape=jax.ShapeDtypeStruct((b, N_HEADS * V_DIM, s), BF16),
        grid=(b, N_HEADS // group),
        in_specs=[
            pl.BlockSpec((1, group * HEAD_PAD, s), lambda i, h: (i, h, 0)),
            pl.BlockSpec((1, s, group * HEAD_PAD), lambda i, h: (i, 0, h)),
            pl.BlockSpec((1, group * V_DIM, s), lambda i, h: (i, h, 0)),
            pl.BlockSpec((1, group, 1, s), lambda i, h: (i, h, 0, 0)),
            pl.BlockSpec((N_META, group * HEAD_PAD), lambda i, h: (0, h)),
            pl.BlockSpec((group * V_DIM, N_META), lambda i, h: (h, 0)),
        ],
        out_specs=pl.BlockSpec((1, group * V_DIM, s), lambda i, h: (i, h, 0)),
        compiler_params=pltpu.CompilerParams(
            dimension_semantics=("arbitrary", "arbitrary"), vmem_limit_bytes=V7X_VMEM_LIMIT_BYTES),
        name="attention",
    )(qT, k, vT, dg, k_meta, vT_meta)


def _out_ffn_kernel(oT_ref, gT_ref, h_ref, wbfT_ref, wbmT_ref, woT_ref, gain_ref, wgu_ref, wd_ref, o_ref, *, sub):
    tm = h_ref.shape[1]
    d = woT_ref.shape[0]
    n_sub = tm // sub

    def branches(j):
        lanes = slice(j * sub, (j + 1) * sub)
        yf = jnp.dot(wbfT_ref[...], oT_ref[0, :FOX_W, lanes], preferred_element_type=F32)
        ym = jnp.dot(wbmT_ref[...], oT_ref[0, FOX_W:, lanes], preferred_element_type=F32)
        gf = gT_ref[0, :d, lanes].astype(F32)
        gm = gT_ref[0, d:, lanes].astype(F32)
        return (gf * yf + gm * ym).astype(BF16)

    def mixed(j, z):
        mix = lax.dot_general(z, woT_ref[...], _TN, preferred_element_type=F32)
        return h_ref[0, j * sub:(j + 1) * sub, :] + mix

    zs = [branches(j) for j in range(n_sub)]
    h2 = [mixed(j, zs[j]) for j in range(n_sub)]
    gu = [_gate_up(h2[j], gain_ref[...], wgu_ref) for j in range(n_sub)]
    for j in range(n_sub):
        o_ref[0, j * sub:(j + 1) * sub, :] = _half_ffn_out(h2[j], *gu[j], wd_ref)


def _out_ffn(oT, gT, h3d, wbfT, wbmT, woT, gain, w_gu, w_down, *, tm, sub):
    b, s, d = h3d.shape
    return pl.pallas_call(
        functools.partial(_out_ffn_kernel, sub=sub),
        out_shape=jax.ShapeDtypeStruct((b, s, d), F32),
        grid=(b, s // tm),
        in_specs=[
            pl.BlockSpec((1, oT.shape[1], tm), lambda i, t: (i, 0, t)),
            pl.BlockSpec((1, gT.shape[1], tm), lambda i, t: (i, 0, t)),
            pl.BlockSpec((1, tm, d), lambda i, t: (i, t, 0)),
            _const_spec(wbfT.shape),
            _const_spec(wbmT.shape),
            _const_spec(woT.shape),
            _const_spec(gain.shape),
            _const_spec(w_gu.shape),
            _const_spec(w_down.shape),
        ],
        out_specs=pl.BlockSpec((1, tm, d), lambda i, t: (i, t, 0)),
        compiler_params=pltpu.CompilerParams(
            dimension_semantics=("arbitrary", "arbitrary"), vmem_limit_bytes=V7X_VMEM_LIMIT_BYTES),
        name="out_ffn",
    )(oT, gT, h3d, wbfT, wbmT, woT, gain, w_gu, w_down)


def _col(v):
    return v.astype(F32).reshape(-1, 1)


def _pick_tile(n, pref):
    t = min(n, pref)
    while n % t:
        t //= 2
    return t


def kernel(x, meta_tokens, ffn1_norm, ffn1_w_gu, ffn1_w_down, mix_norm, w_in, b_forget, b_gate, fox_q_norm,
           fox_k_norm, mla_cq_norm, mla_w_uq, mla_ckv_norm, mla_w_ukv, mla_q_norm, mla_k_norm, w_branch_fox,
           w_branch_mla, w_out, ffn2_norm, ffn2_w_gu, ffn2_w_down):
    b, s, d = x.shape
    depth = ffn1_norm.shape[0]
    if depth != 1:
        raise NotImplementedError("only depth 1 is supported")
    meta_rows = 128

    pos = jnp.arange(N_META + s, dtype=F32)
    inv_freq = ROPE_THETA ** (-jnp.arange(0, MLA_ROPE, 2, dtype=F32) / MLA_ROPE)
    ang = pos[:, None] * inv_freq[None, :]
    cosT, sinT = jnp.cos(ang).T, jnp.sin(ang).T
    pad = ((0, 0), (0, meta_rows - N_META))
    cos_meta, sin_meta = jnp.pad(cosT[:, :N_META], pad), jnp.pad(sinT[:, :N_META], pad)
    cos_real, sin_real = cosT[:, N_META:], sinT[:, N_META:]

    tm_fused = _pick_tile(s, 256)
    tm_out = _pick_tile(s, 512)
    sub_out = _pick_tile(tm_out, 256)
    tq = _pick_tile(s, 256)

    w1gu, w1d = ffn1_w_gu[0].astype(BF16), ffn1_w_down[0].astype(BF16)
    w2gu, w2d = ffn2_w_gu[0].astype(BF16), ffn2_w_down[0].astype(BF16)
    g1, g2 = ffn1_norm[0].reshape(1, d).astype(F32), ffn2_norm[0].reshape(1, d).astype(F32)
    p = {
        "mix_norm": mix_norm[0].reshape(1, d).astype(F32),
        "w_lowT": w_in[0][:, OFF_FL:OFF_GATE].T.astype(BF16),
        "w_wideT": jnp.concatenate([w_in[0][:, :OFF_FL], w_in[0][:, OFF_GATE:]], axis=1).T.astype(BF16),
        "b_forget": _col(b_forget[0]),
        "b_gate": _col(b_gate[0]),
        "fox_q_norm": _col(fox_q_norm[0]),
        "fox_k_norm": _col(fox_k_norm[0]),
        "mla_cq_norm": _col(mla_cq_norm[0]),
        "mla_w_uqT": mla_w_uq[0].T.astype(BF16),
        "mla_ckv_norm": _col(mla_ckv_norm[0]),
        "mla_w_ukvT": mla_w_ukv[0].T.astype(BF16),
        "mla_q_norm": _col(mla_q_norm[0]),
        "mla_k_norm": _col(mla_k_norm[0]),
    }
    wbfT = w_branch_fox[0].T.astype(BF16)
    wbmT = w_branch_mla[0].T.astype(BF16)
    woT = w_out[0].astype(BF16)

    hm = jnp.pad(meta_tokens.astype(F32), ((0, meta_rows - N_META), (0, 0)))
    hm1 = _ffn(hm, g1, w1gu, w1d, tm=meta_rows)
    _, km, vmT, _, _ = _proj_meta(hm1[None], cos_meta, sin_meta, p)
    k_meta, vT_meta = km[0, :N_META], vmT[0, :, :N_META]

    h1, qT, k, vT, gT, dg = _ffn_proj(x.astype(F32).reshape(b * s, d), g1, w1gu, w1d, cos_real, sin_real, p,
                                  b=b, s=s, tm=tm_fused)
    h1 = h1.reshape(b, s, d)
    oT = _attention(qT, k, vT, dg, k_meta, vT_meta, tq=tq, group=ATTN_GROUP)
    h3 = _out_ffn(oT, gT, h1, wbfT, wbmT, woT, g2, w2gu, w2d, tm=tm_out, sub=sub_out)
    return h3.astype(x.dtype)
```

```python
import collections
import functools

import jax
import jax.numpy as jnp
from jax import lax
from jax.experimental import pallas as pl
from jax.experimental.pallas import tpu as pltpu

F32 = jnp.float32
BF16 = jnp.bfloat16

EPS = 1e-6
N_META = 16
FOX_HEADS = 8
FOX_DIM = 64
FOX_W = FOX_HEADS * FOX_DIM
MLA_HEADS = 8
MLA_Q_RANK = 256
MLA_KV_RANK = 128
MLA_NOPE = 64
MLA_ROPE = 32
MLA_QK = MLA_NOPE + MLA_ROPE
MLA_V = 64
ROPE_THETA = 10000.0
LOG2E = 1.4426950408889634

N_HEADS = FOX_HEADS + MLA_HEADS
HEAD_PAD = 128
V_DIM = 64

OFF_FQ = 0
OFF_FK = OFF_FQ + FOX_W
OFF_FV = OFF_FK + FOX_W
OFF_FL = OFF_FV + FOX_W
OFF_CQ = OFF_FL + FOX_HEADS
OFF_CKV = OFF_CQ + MLA_Q_RANK
OFF_KR = OFF_CKV + MLA_KV_RANK
OFF_GATE = OFF_KR + MLA_ROPE
LOW_FL = 0
LOW_CQ = LOW_FL + FOX_HEADS
LOW_CKV = LOW_CQ + MLA_Q_RANK
LOW_KR = LOW_CKV + MLA_KV_RANK
LOW_ROWS = LOW_KR + MLA_ROPE
WIDE_FQ = 0
WIDE_FK = WIDE_FQ + FOX_W
WIDE_FV = WIDE_FK + FOX_W
WIDE_GATE = WIDE_FV + FOX_W

V7X_VMEM_LIMIT_BYTES = 56 * 1024 * 1024
MASK_VALUE = -1e30
ATTN_GROUP = 8
ATTN_LOOKAHEAD = 2
GATE_SPLIT_EIGHTHS = (0, 2, 4, 7, 8)
SHIFT_ROW = 104

_NT = (((1,), (1,)), ((), ()))
_TN = (((0,), (0,)), ((), ()))


def _const_spec(shape):
    zeros = (0,) * len(shape)
    return pl.BlockSpec(shape, lambda *_: zeros, pipeline_mode=pl.Buffered(1))


def _gate_up(x, gain, wgu_ref):
    d_ff = wgu_ref.shape[1] // 2
    ms = jnp.mean(x * x, axis=-1, keepdims=True)
    u = (x * lax.rsqrt(ms + EPS) * gain).astype(BF16)
    g = jnp.dot(u, wgu_ref[:, :d_ff], preferred_element_type=F32)
    up = jnp.dot(u, wgu_ref[:, d_ff:], preferred_element_type=F32)
    return g, up


def _half_ffn_out(x, g, up, wd_ref):
    a = (g * jax.nn.sigmoid(g) * up).astype(BF16)
    return x + 0.5 * jnp.dot(a, wd_ref[...], preferred_element_type=F32)


def _ffn_kernel(x_ref, gain_ref, wgu_ref, wd_ref, o_ref):
    x = x_ref[...]
    g, up = _gate_up(x, gain_ref[...], wgu_ref)
    o_ref[...] = _half_ffn_out(x, g, up, wd_ref)


def _ffn(x2d, gain, w_gu, w_down, *, tm):
    n, d = x2d.shape
    d_ff = w_down.shape[0]
    return pl.pallas_call(
        _ffn_kernel,
        out_shape=jax.ShapeDtypeStruct((n, d), F32),
        grid=(n // tm,),
        in_specs=[
            pl.BlockSpec((tm, d), lambda i: (i, 0)),
            _const_spec((1, d)),
            _const_spec((d, 2 * d_ff)),
            _const_spec((d_ff, d)),
        ],
        out_specs=pl.BlockSpec((tm, d), lambda i: (i, 0)),
        compiler_params=pltpu.CompilerParams(
            dimension_semantics=("arbitrary",), vmem_limit_bytes=V7X_VMEM_LIMIT_BYTES),
        name="ffn",
    )(x2d, gain, w_gu, w_down)


def _bf16_parts(c):
    hi = c.astype(BF16).astype(F32)
    r = c - hi
    mid = r.astype(BF16).astype(F32)
    lo = (r - mid).astype(BF16).astype(F32)
    return hi, mid, lo


def _rows8(vals, tm):
    row = lax.broadcasted_iota(jnp.int32, (8, tm), 0)
    out = jnp.zeros((8, tm), F32)
    for j, v in enumerate(vals):
        out = jnp.where(row == j, v, out)
    return out


def _key_tail(one_rows, first_row, tm):
    return jnp.concatenate([jnp.zeros((SHIFT_ROW - first_row, tm), F32), one_rows,
                            jnp.zeros((HEAD_PAD - SHIFT_ROW - 8, tm), F32)], axis=0)


def _rms_rows(x, n):
    return lax.rsqrt(jnp.sum(x * x, axis=0, keepdims=True) * (1.0 / n) + EPS)


def _rope_rows(x, cos, sin):
    x1 = x[:MLA_ROPE // 2]
    x2 = x[MLA_ROPE // 2:]
    return x1 * cos - x2 * sin, x1 * sin + x2 * cos


_ProjRefs = collections.namedtuple(
    "_ProjRefs", "gain wlowT wwideT bf bg gq gk gcq wuqT gckv wukvT gmq gmk cos sin qT k vT gT dg")


class _ProjOps:
    def __init__(self, u, r, tm, meta):
        self.u, self.r, self.tm, self.meta = u, r, tm, meta
        n_gate = r.wwideT.shape[0] - WIDE_GATE
        self.bounds = [WIDE_GATE + n_gate * e // 8 for e in GATE_SPLIT_EIGHTHS]

    def _wide_rows(self, r0, r1):
        return lax.dot_general(self.r.wwideT[r0:r1, :], self.u, _NT, preferred_element_type=F32)

    def low(self):
        return lax.dot_general(self.r.wlowT[...], self.u, _NT, preferred_element_type=F32)

    def fox(self):
        return self._wide_rows(0, WIDE_GATE)

    def gate_logits(self, j):
        return self._wide_rows(self.bounds[j], self.bounds[j + 1])

    def store_gates(self, j, logits):
        rows = slice(self.bounds[j] - WIDE_GATE, self.bounds[j + 1] - WIDE_GATE)
        self.r.gT[0, rows, :] = jax.nn.sigmoid(logits + self.r.bg[rows, :]).astype(BF16)

    def forget_cumsum(self, low, carry):
        tm = self.tm
        fl = low[LOW_FL:LOW_FL + FOX_HEADS] + self.r.bf[...]
        lf = jnp.minimum(fl, 0.0) - jnp.log1p(jnp.exp(-jnp.abs(fl)))
        r_i = lax.broadcasted_iota(jnp.int32, (tm, tm), 0)
        c_i = lax.broadcasted_iota(jnp.int32, (tm, tm), 1)
        upper = jnp.where(r_i <= c_i, 1.0, 0.0).astype(BF16)
        cs = None
        for part in _bf16_parts(lf):
            d = jnp.dot(part.astype(BF16), upper, preferred_element_type=F32)
            cs = d if cs is None else cs + d
        if self.meta:
            return (cs - cs[:, N_META - 1:N_META]) * LOG2E, None
        return (cs + carry) * LOG2E, carry + cs[:, tm - 1:tm]

    def mla_up(self, low):
        r = self.r
        cq = low[LOW_CQ:LOW_CQ + MLA_Q_RANK]
        cqn = (cq * _rms_rows(cq, MLA_Q_RANK) * r.gcq[...]).astype(BF16)
        qm = jnp.dot(r.wuqT[...], cqn, preferred_element_type=F32)
        ckv = low[LOW_CKV:LOW_CKV + MLA_KV_RANK]
        ckvn = (ckv * _rms_rows(ckv, MLA_KV_RANK) * r.gckv[...]).astype(BF16)
        kv = jnp.dot(r.wukvT[...], ckvn, preferred_element_type=F32)
        return qm, kv

    def store_mla_heads(self, low, qm, kv):
        r, tm = self.r, self.tm
        kr = low[LOW_KR:LOW_KR + MLA_ROPE]
        ss_kr = jnp.sum(kr * kr, axis=0, keepdims=True)
        cos = r.cos[...]
        sin = r.sin[...]
        gmq = r.gmq[...]
        gmk = r.gmk[...]
        zpad = jnp.zeros((HEAD_PAD - MLA_QK, tm), F32)
        ones = jnp.ones((1, tm), F32)
        ktail = _key_tail(_rows8([ones, ones, ones], tm), MLA_QK, tm)
        for hh in range(MLA_HEADS):
            qh = qm[hh * MLA_QK:(hh + 1) * MLA_QK]
            qn = qh * (_rms_rows(qh, MLA_QK) * (MLA_QK ** -0.5 * LOG2E)) * gmq
            q1, q2 = _rope_rows(qn[MLA_NOPE:], cos, sin)
            q_ext = jnp.concatenate([qn[:MLA_NOPE], q1, q2, zpad], axis=0)
            kn_raw = kv[hh * (MLA_NOPE + MLA_V):hh * (MLA_NOPE + MLA_V) + MLA_NOPE]
            rk = lax.rsqrt((jnp.sum(kn_raw * kn_raw, axis=0, keepdims=True) + ss_kr) * (1.0 / MLA_QK) + EPS)
            kn = kn_raw * rk * gmk[:MLA_NOPE]
            k1, k2 = _rope_rows(kr * rk * gmk[MLA_NOPE:], cos, sin)
            k_ext = jnp.concatenate([kn, k1, k2, ktail], axis=0)
            g = FOX_HEADS + hh
            r.qT[0, g * HEAD_PAD:(g + 1) * HEAD_PAD, :] = q_ext.astype(BF16)
            r.k[0, :, g * HEAD_PAD:(g + 1) * HEAD_PAD] = k_ext.T.astype(BF16)
            r.dg[0, g] = jnp.sum(q_ext * k_ext, axis=0, keepdims=True)
            r.vT[0, g * V_DIM:(g + 1) * V_DIM, :] = (
                kv[hh * (MLA_NOPE + MLA_V) + MLA_NOPE:(hh + 1) * (MLA_NOPE + MLA_V)].astype(BF16))

    def store_fox_heads(self, fox, c):
        r, tm = self.r, self.tm
        ones = jnp.ones((1, tm), F32)
        one_rows = _rows8([ones, ones, ones], tm)
        zpad = jnp.zeros((HEAD_PAD - FOX_DIM - 16, tm), F32)
        ktail = _key_tail(one_rows, FOX_DIM + 16, tm)
        gq = r.gq[...]
        gk = r.gk[...]
        for hh in range(FOX_HEADS):
            qh = fox[WIDE_FQ + hh * FOX_DIM:WIDE_FQ + (hh + 1) * FOX_DIM]
            kh = fox[WIDE_FK + hh * FOX_DIM:WIDE_FK + (hh + 1) * FOX_DIM]
            qn = qh * (_rms_rows(qh, FOX_DIM) * (FOX_DIM ** -0.5 * LOG2E)) * gq
            kn = kh * _rms_rows(kh, FOX_DIM) * gk
            c_rows = _rows8(list(_bf16_parts(c[hh:hh + 1])), tm)
            q_ext = jnp.concatenate([qn, c_rows, one_rows, zpad], axis=0)
            k_ext = jnp.concatenate([kn, one_rows, -c_rows, ktail], axis=0)
            r.qT[0, hh * HEAD_PAD:(hh + 1) * HEAD_PAD, :] = q_ext.astype(BF16)
            r.k[0, :, hh * HEAD_PAD:(hh + 1) * HEAD_PAD] = k_ext.T.astype(BF16)
            r.dg[0, hh] = jnp.sum(q_ext * k_ext, axis=0, keepdims=True)
            r.vT[0, hh * V_DIM:(hh + 1) * V_DIM, :] = (
                fox[WIDE_FV + hh * FOX_DIM:WIDE_FV + (hh + 1) * FOX_DIM].astype(BF16))


def _rms_norm_bf16(x, gain):
    ms = jnp.mean(x * x, axis=-1, keepdims=True)
    return (x * lax.rsqrt(ms + EPS) * gain).astype(BF16)


def _proj_meta_kernel(h_ref, *refs):
    r = _ProjRefs(*refs)
    tm = h_ref.shape[1]
    ops = _ProjOps(_rms_norm_bf16(h_ref[0], r.gain[...]), r, tm, meta=True)
    low = ops.low()
    fox = ops.fox()
    c, _ = ops.forget_cumsum(low, None)
    qm, kv = ops.mla_up(low)
    ops.store_fox_heads(fox, c)
    ops.store_mla_heads(low, qm, kv)
    for j in range(len(ops.bounds) - 1):
        ops.store_gates(j, ops.gate_logits(j))


def _ffn_proj_kernel(x_ref, fgain_ref, wgu_ref, wd_ref, *refs, tiles_per_row):
    *proj_refs, h1_ref, qT_ref, k_ref, vT_ref, gT_ref, dg_ref, hprev_ref, carry_ref = refs
    r = _ProjRefs(*proj_refs, qT_ref, k_ref, vT_ref, gT_ref, dg_ref)
    t = pl.program_id(0)
    tm = x_ref.shape[0]

    @pl.when(t == 0)
    def _():
        hprev_ref[...] = jnp.zeros_like(hprev_ref)
        carry_ref[...] = jnp.zeros_like(carry_ref)

    ops = _ProjOps(_rms_norm_bf16(hprev_ref[...], r.gain[...]), r, tm, meta=False)
    first_of_row = lax.rem(t + (tiles_per_row - 1), tiles_per_row) == 0
    carry = jnp.where(first_of_row, 0.0, carry_ref[:, 0:1])
    x = x_ref[...]

    low = ops.low()
    g, up = _gate_up(x, fgain_ref[...], wgu_ref)
    c, carry = ops.forget_cumsum(low, carry)
    qm, kv = ops.mla_up(low)
    fox = ops.fox()
    ops.store_mla_heads(low, qm, kv)
    h1 = _half_ffn_out(x, g, up, wd_ref)
    ops.store_fox_heads(fox, c)
    n_chunks = len(ops.bounds) - 1
    gate = ops.gate_logits(0)
    for j in range(n_chunks):
        gate_next = ops.gate_logits(j + 1) if j + 1 < n_chunks else None
        ops.store_gates(j, gate)
        gate = gate_next
    h1_ref[...] = h1
    hprev_ref[...] = h1
    carry_ref[...] = jnp.broadcast_to(carry, carry_ref.shape)


def _proj_consts(p):
    return [p["mix_norm"], p["w_lowT"], p["w_wideT"], p["b_forget"], p["b_gate"], p["fox_q_norm"],
            p["fox_k_norm"], p["mla_cq_norm"], p["mla_w_uqT"], p["mla_ckv_norm"], p["mla_w_ukvT"],
            p["mla_q_norm"], p["mla_k_norm"]]


def _proj_out_shapes(b, s, n_gate):
    return (
        jax.ShapeDtypeStruct((b, N_HEADS * HEAD_PAD, s), BF16),
        jax.ShapeDtypeStruct((b, s, N_HEADS * HEAD_PAD), BF16),
        jax.ShapeDtypeStruct((b, N_HEADS * V_DIM, s), BF16),
        jax.ShapeDtypeStruct((b, n_gate, s), BF16),
        jax.ShapeDtypeStruct((b, N_HEADS, 1, s), F32),
    )


def _proj_meta(h3d, cosT, sinT, p):
    _, tm, d = h3d.shape
    n_gate = p["w_wideT"].shape[0] - WIDE_GATE
    consts = _proj_consts(p)
    whole = lambda shape: pl.BlockSpec(shape, lambda i: (0,) * len(shape))
    return pl.pallas_call(
        _proj_meta_kernel,
        out_shape=_proj_out_shapes(1, tm, n_gate),
        grid=(1,),
        in_specs=[whole(h3d.shape)] + [whole(c.shape) for c in consts] + [whole(cosT.shape), whole(sinT.shape)],
        out_specs=tuple(whole(o.shape) for o in _proj_out_shapes(1, tm, n_gate)),
        compiler_params=pltpu.CompilerParams(
            dimension_semantics=("arbitrary",), vmem_limit_bytes=V7X_VMEM_LIMIT_BYTES),
        name="proj_meta",
    )(h3d, *consts, cosT, sinT)


def _ffn_proj(x2d, fgain, w_gu, w_down, cosT, sinT, p, *, b, s, tm):
    n, d = x2d.shape
    n_gate = p["w_wideT"].shape[0] - WIDE_GATE
    consts = _proj_consts(p)
    tiles_per_row = s // tm
    n_tiles = n // tm

    def cur(t):
        return jnp.minimum(t, n_tiles - 1)

    def prev_row(t):
        tt = jnp.maximum(t - 1, 0)
        return tt // tiles_per_row, tt % tiles_per_row

    outs = pl.pallas_call(
        functools.partial(_ffn_proj_kernel, tiles_per_row=tiles_per_row),
        out_shape=(jax.ShapeDtypeStruct((n, d), F32),) + _proj_out_shapes(b, s, n_gate),
        grid=(n_tiles + 1,),
        in_specs=[
            pl.BlockSpec((tm, d), lambda t: (cur(t), 0)),
            _const_spec(fgain.shape),
            _const_spec(w_gu.shape),
            _const_spec(w_down.shape),
        ] + [_const_spec(c.shape) for c in consts]
        + [pl.BlockSpec((MLA_ROPE // 2, tm), lambda t: (0, prev_row(t)[1]))] * 2,
        out_specs=(
            pl.BlockSpec((tm, d), lambda t: (cur(t), 0)),
            pl.BlockSpec((1, N_HEADS * HEAD_PAD, tm), lambda t: (prev_row(t)[0], 0, prev_row(t)[1])),
            pl.BlockSpec((1, tm, N_HEADS * HEAD_PAD), lambda t: (prev_row(t)[0], prev_row(t)[1], 0)),
            pl.BlockSpec((1, N_HEADS * V_DIM, tm), lambda t: (prev_row(t)[0], 0, prev_row(t)[1])),
            pl.BlockSpec((1, n_gate, tm), lambda t: (prev_row(t)[0], 0, prev_row(t)[1])),
            pl.BlockSpec((1, N_HEADS, 1, tm), lambda t: (prev_row(t)[0], 0, 0, prev_row(t)[1])),
        ),
        scratch_shapes=[pltpu.VMEM((tm, d), F32), pltpu.VMEM((FOX_HEADS, 128), F32)],
        compiler_params=pltpu.CompilerParams(
            dimension_semantics=("arbitrary",), vmem_limit_bytes=V7X_VMEM_LIMIT_BYTES),
        name="ffn_proj",
    )(x2d, fgain, w_gu, w_down, *consts, cosT, sinT)
    return outs


def _attn_kernel(qT_ref, k_ref, vT_ref, dg_ref, km_ref, vmT_ref, o_ref, *, tq, group):
    s_len = qT_ref.shape[2]
    row = lax.broadcasted_iota(jnp.int32, (tq, tq), 0)
    col = lax.broadcasted_iota(jnp.int32, (tq, tq), 1)
    causal = row <= col
    both = lax.broadcasted_iota(jnp.int32, (tq, 2 * tq), 1)
    row2 = lax.broadcasted_iota(jnp.int32, (tq, 2 * tq), 0)
    causal_pair = (row2 <= both) | (both >= tq)
    ones_real = jnp.ones((16, s_len), BF16)
    ones_meta = jnp.ones((16, N_META), BF16)
    v_ext = [jnp.concatenate([vT_ref[0, g * V_DIM:(g + 1) * V_DIM, :], ones_real], axis=0)
             for g in range(group)]
    vm_ext = [jnp.concatenate([vmT_ref[g * V_DIM:(g + 1) * V_DIM, :], ones_meta], axis=0)
              for g in range(group)]
    q_tail_zeros = jnp.zeros((HEAD_PAD - SHIFT_ROW - 8, 2 * tq), F32)
    q_head_zeros = jnp.zeros((SHIFT_ROW - MLA_QK, 2 * tq), F32)

    def meta_scores(i, g, exact):
        head = slice(g * HEAD_PAD, (g + 1) * HEAD_PAD)
        qT = qT_ref[0, head, 2 * i * tq:(2 * i + 2) * tq]
        sm = jnp.dot(km_ref[:, head], qT, preferred_element_type=F32)
        shift = None
        if not exact:
            own = dg_ref[0, g, :, 2 * i * tq:(2 * i + 2) * tq]
            shift = jnp.maximum(jnp.max(sm, axis=0, keepdims=True), own)
            parts = _rows8([-part for part in _bf16_parts(shift)], 2 * tq)
            tail = jnp.concatenate([q_head_zeros, parts, q_tail_zeros], axis=0).astype(BF16)
            qT = jnp.concatenate([qT[:MLA_QK], tail], axis=0)
        return qT, sm, shift

    def scores(i, g, qT, sm, shift):
        head = slice(g * HEAD_PAD, (g + 1) * HEAD_PAD)
        n1 = (2 * i + 1) * tq
        s_main = jnp.dot(k_ref[0, :n1, head], qT, preferred_element_type=F32)
        s_low = jnp.dot(k_ref[0, n1:n1 + tq, head], qT[:, tq:], preferred_element_type=F32)
        return s_main, s_low, sm, shift

    def finish(i, g, s_main, s_low, sm, shift, exact):
        n1 = (2 * i + 1) * tq
        diag = jnp.where(causal_pair, s_main[n1 - tq:], MASK_VALUE)
        low = jnp.where(causal, s_low, MASK_VALUE)
        top = s_main[:n1 - tq] if i > 0 else None
        if exact:
            m = jnp.maximum(jnp.max(diag, axis=0, keepdims=True), jnp.max(sm, axis=0, keepdims=True))
            if i > 0:
                m = jnp.maximum(m, jnp.max(top, axis=0, keepdims=True))
            m = jnp.concatenate([m[:, :tq], jnp.maximum(m[:, tq:], jnp.max(low, axis=0, keepdims=True))],
                                axis=1)
            diag, low, sm = diag - m, low - m[:, tq:], sm - m
            top = top - m if i > 0 else None
        else:
            sm = sm - shift
        parts = [jnp.exp2(diag).astype(BF16)]
        if i > 0:
            parts = [jnp.exp2(top).astype(BF16)] + parts
        p = jnp.concatenate(parts, axis=0) if len(parts) > 1 else parts[0]
        p_low = jnp.exp2(low).astype(BF16)
        pm = jnp.exp2(sm).astype(BF16)
        o = (jnp.dot(v_ext[g][:, :n1], p, preferred_element_type=F32)
             + jnp.dot(vm_ext[g], pm, preferred_element_type=F32))
        o_hi = o[:, tq:] + jnp.dot(v_ext[g][:, n1:n1 + tq], p_low, preferred_element_type=F32)
        den = jnp.concatenate([o[V_DIM:V_DIM + 1, :tq], o_hi[V_DIM:V_DIM + 1]], axis=1)
        rows = slice(g * V_DIM, (g + 1) * V_DIM)
        o_ref[0, rows, 2 * i * tq:(2 * i + 1) * tq] = (o[:V_DIM, :tq] * (1.0 / den[:, :tq])).astype(BF16)
        o_ref[0, rows, (2 * i + 1) * tq:(2 * i + 2) * tq] = (o_hi[:V_DIM] * (1.0 / den[:, tq:])).astype(BF16)
        usable = (den > 0.0) & (den < jnp.inf)
        return jnp.where(usable, 0.0, 1.0)

    nq = s_len // (2 * tq)
    order = [t for pair in zip(range(nq - 1, -1, -1), range(nq)) for t in pair][:nq]

    def run(exact):
        flags = jnp.zeros((1, 2 * tq), F32)
        meta = {(i, g): meta_scores(i, g, exact) for i in order for g in range(group)}
        pending = []
        for i in order:
            for g in range(group):
                pending.append((i, g) + scores(i, g, *meta[i, g]))
                if len(pending) > ATTN_LOOKAHEAD:
                    flags = jnp.maximum(flags, finish(*pending.pop(0), exact))
        for unit in pending:
            flags = jnp.maximum(flags, finish(*unit, exact))
        return flags

    flags = run(exact=False)

    @pl.when(jnp.max(flags) > 0.0)
    def _():
        run(exact=True)


def _attention(qT, k, vT, dg, k_meta, vT_meta, *, tq, group):
    b, _, s = qT.shape
    return pl.pallas_call(
        functools.partial(_attn_kernel, tq=tq, group=group),
        out_shape=jax.ShapeDtypeStruct((b, N_HEADS * V_DIM, s), BF16),
        grid=(b, N_HEADS // group),
        in_specs=[
            pl.BlockSpec((1, group * HEAD_PAD, s), lambda i, h: (i, h, 0)),
            pl.BlockSpec((1, s, group * HEAD_PAD), lambda i, h: (i, 0, h)),
            pl.BlockSpec((1, group * V_DIM, s), lambda i, h: (i, h, 0)),
            pl.BlockSpec((1, group, 1, s), lambda i, h: (i, h, 0, 0)),
            pl.BlockSpec((N_META, group * HEAD_PAD), lambda i, h: (0, h)),
            pl.BlockSpec((group * V_DIM, N_META), lambda i, h: (h, 0)),
        ],
        out_specs=pl.BlockSpec((1, group * V_DIM, s), lambda i, h: (i, h, 0)),
        compiler_params=pltpu.CompilerParams(
            dimension_semantics=("arbitrary", "arbitrary"), vmem_limit_bytes=V7X_VMEM_LIMIT_BYTES),
        name="attention",
    )(qT, k, vT, dg, k_meta, vT_meta)


def _out_ffn_kernel(oT_ref, gT_ref, h_ref, wbfT_ref, wbmT_ref, woT_ref, gain_ref, wgu_ref, wd_ref, o_ref, *, sub):
    tm = h_ref.shape[1]
    d = woT_ref.shape[0]
    n_sub = tm // sub

    def branches(j):
        lanes = slice(j * sub, (j + 1) * sub)
        yf = jnp.dot(wbfT_ref[...], oT_ref[0, :FOX_W, lanes], preferred_element_type=F32)
        ym = jnp.dot(wbmT_ref[...], oT_ref[0, FOX_W:, lanes], preferred_element_type=F32)
        gf = gT_ref[0, :d, lanes].astype(F32)
        gm = gT_ref[0, d:, lanes].astype(F32)
        return (gf * yf + gm * ym).astype(BF16)

    def mixed(j, z):
        mix = lax.dot_general(z, woT_ref[...], _TN, preferred_element_type=F32)
        return h_ref[0, j * sub:(j + 1) * sub, :] + mix

    zs = [branches(j) for j in range(n_sub)]
    h2 = [mixed(j, zs[j]) for j in range(n_sub)]
    gu = [_gate_up(h2[j], gain_ref[...], wgu_ref) for j in range(n_sub)]
    for j in range(n_sub):
        o_ref[0, j * sub:(j + 1) * sub, :] = _half_ffn_out(h2[j], *gu[j], wd_ref)


def _out_ffn(oT, gT, h3d, wbfT, wbmT, woT, gain, w_gu, w_down, *, tm, sub):
    b, s, d = h3d.shape
    return pl.pallas_call(
        functools.partial(_out_ffn_kernel, sub=sub),
        out_shape=jax.ShapeDtypeStruct((b, s, d), F32),
        grid=(b, s // tm),
        in_specs=[
            pl.BlockSpec((1, oT.shape[1], tm), lambda i, t: (i, 0, t)),
            pl.BlockSpec((1, gT.shape[1], tm), lambda i, t: (i, 0, t)),
            pl.BlockSpec((1, tm, d), lambda i, t: (i, t, 0)),
            _const_spec(wbfT.shape),
            _const_spec(wbmT.shape),
            _const_spec(woT.shape),
            _const_spec(gain.shape),
            _const_spec(w_gu.shape),
            _const_spec(w_down.shape),
        ],
        out_specs=pl.BlockSpec((1, tm, d), lambda i, t: (i, t, 0)),
        compiler_params=pltpu.CompilerParams(
            dimension_semantics=("arbitrary", "arbitrary"), vmem_limit_bytes=V7X_VMEM_LIMIT_BYTES),
        name="out_ffn",
    )(oT, gT, h3d, wbfT, wbmT, woT, gain, w_gu, w_down)


def _col(v):
    return v.astype(F32).reshape(-1, 1)


def _pick_tile(n, pref):
    t = min(n, pref)
    while n % t:
        t //= 2
    return t


def kernel(x, meta_tokens, ffn1_norm, ffn1_w_gu, ffn1_w_down, mix_norm, w_in, b_forget, b_gate, fox_q_norm,
           fox_k_norm, mla_cq_norm, mla_w_uq, mla_ckv_norm, mla_w_ukv, mla_q_norm, mla_k_norm, w_branch_fox,
           w_branch_mla, w_out, ffn2_norm, ffn2_w_gu, ffn2_w_down):
    b, s, d = x.shape
    depth = ffn1_norm.shape[0]
    if depth != 1:
        raise NotImplementedError("only depth 1 is supported")
    meta_rows = 128

    pos = jnp.arange(N_META + s, dtype=F32)
    inv_freq = ROPE_THETA ** (-jnp.arange(0, MLA_ROPE, 2, dtype=F32) / MLA_ROPE)
    ang = pos[:, None] * inv_freq[None, :]
    cosT, sinT = jnp.cos(ang).T, jnp.sin(ang).T
    pad = ((0, 0), (0, meta_rows - N_META))
    cos_meta, sin_meta = jnp.pad(cosT[:, :N_META], pad), jnp.pad(sinT[:, :N_META], pad)
    cos_real, sin_real = cosT[:, N_META:], sinT[:, N_META:]

    tm_fused = _pick_tile(s, 256)
    tm_out = _pick_tile(s, 512)
    sub_out = _pick_tile(tm_out, 256)
    tq = _pick_tile(s, 256)

    w1gu, w1d = ffn1_w_gu[0].astype(BF16), ffn1_w_down[0].astype(BF16)
    w2gu, w2d = ffn2_w_gu[0].astype(BF16), ffn2_w_down[0].astype(BF16)
    g1, g2 = ffn1_norm[0].reshape(1, d).astype(F32), ffn2_norm[0].reshape(1, d).astype(F32)
    p = {
        "mix_norm": mix_norm[0].reshape(1, d).astype(F32),
        "w_lowT": w_in[0][:, OFF_FL:OFF_GATE].T.astype(BF16),
        "w_wideT": jnp.concatenate([w_in[0][:, :OFF_FL], w_in[0][:, OFF_GATE:]], axis=1).T.astype(BF16),
        "b_forget": _col(b_forget[0]),
        "b_gate": _col(b_gate[0]),
        "fox_q_norm": _col(fox_q_norm[0]),
        "fox_k_norm": _col(fox_k_norm[0]),
        "mla_cq_norm": _col(mla_cq_norm[0]),
        "mla_w_uqT": mla_w_uq[0].T.astype(BF16),
        "mla_ckv_norm": _col(mla_ckv_norm[0]),
        "mla_w_ukvT": mla_w_ukv[0].T.astype(BF16),
        "mla_q_norm": _col(mla_q_norm[0]),
        "mla_k_norm": _col(mla_k_norm[0]),
    }
    wbfT = w_branch_fox[0].T.astype(BF16)
    wbmT = w_branch_mla[0].T.astype(BF16)
    woT = w_out[0].astype(BF16)

    hm = jnp.pad(meta_tokens.astype(F32), ((0, meta_rows - N_META), (0, 0)))
    hm1 = _ffn(hm, g1, w1gu, w1d, tm=meta_rows)
    _, km, vmT, _, _ = _proj_meta(hm1[None], cos_meta, sin_meta, p)
    k_meta, vT_meta = km[0, :N_META], vmT[0, :, :N_META]

    h1, qT, k, vT, gT, dg = _ffn_proj(x.astype(F32).reshape(b * s, d), g1, w1gu, w1d, cos_real, sin_real, p,
                                  b=b, s=s, tm=tm_fused)
    h1 = h1.reshape(b, s, d)
    oT = _attention(qT, k, vT, dg, k_meta, vT_meta, tq=tq, group=ATTN_GROUP)
    h3 = _out_ffn(oT, gT, h1, wbfT, wbmT, woT, g2, w2gu, w2d, tm=tm_out, sub=sub_out)
    return h3.astype(x.dtype)
```

```python
import collections
import functools

import jax
import jax.numpy as jnp
from jax import lax
from jax.experimental import pallas as pl
from jax.experimental.pallas import tpu as pltpu

F32 = jnp.float32
BF16 = jnp.bfloat16

EPS = 1e-6
N_META = 16
FOX_HEADS = 8
FOX_DIM = 64
FOX_W = FOX_HEADS * FOX_DIM
MLA_HEADS = 8
MLA_Q_RANK = 256
MLA_KV_RANK = 128
MLA_NOPE = 64
MLA_ROPE = 32
MLA_QK = MLA_NOPE + MLA_ROPE
MLA_V = 64
ROPE_THETA = 10000.0
LOG2E = 1.4426950408889634

N_HEADS = FOX_HEADS + MLA_HEADS
HEAD_PAD = 128
V_DIM = 64

OFF_FQ = 0
OFF_FK = OFF_FQ + FOX_W
OFF_FV = OFF_FK + FOX_W
OFF_FL = OFF_FV + FOX_W
OFF_CQ = OFF_FL + FOX_HEADS
OFF_CKV = OFF_CQ + MLA_Q_RANK
OFF_KR = OFF_CKV + MLA_KV_RANK
OFF_GATE = OFF_KR + MLA_ROPE
LOW_FL = 0
LOW_CQ = LOW_FL + FOX_HEADS
LOW_CKV = LOW_CQ + MLA_Q_RANK
LOW_KR = LOW_CKV + MLA_KV_RANK
LOW_ROWS = LOW_KR + MLA_ROPE
WIDE_FQ = 0
WIDE_FK = WIDE_FQ + FOX_W
WIDE_FV = WIDE_FK + FOX_W
WIDE_GATE = WIDE_FV + FOX_W

V7X_VMEM_LIMIT_BYTES = 56 * 1024 * 1024
MASK_VALUE = -1e30
ATTN_GROUP = 4
ATTN_LOOKAHEAD = 2
GATE_SPLIT_EIGHTHS = (0, 2, 4, 7, 8)
SHIFT_ROW = 104

_NT = (((1,), (1,)), ((), ()))
_TN = (((0,), (0,)), ((), ()))


def _const_spec(shape):
    zeros = (0,) * len(shape)
    return pl.BlockSpec(shape, lambda *_: zeros, pipeline_mode=pl.Buffered(1))


def _gate_up(x, gain, wgu_ref):
    d_ff = wgu_ref.shape[1] // 2
    ms = jnp.mean(x * x, axis=-1, keepdims=True)
    u = (x * lax.rsqrt(ms + EPS) * gain).astype(BF16)
    g = jnp.dot(u, wgu_ref[:, :d_ff], preferred_element_type=F32)
    up = jnp.dot(u, wgu_ref[:, d_ff:], preferred_element_type=F32)
    return g, up


def _half_ffn_out(x, g, up, wd_ref):
    a = (g * jax.nn.sigmoid(g) * up).astype(BF16)
    return x + 0.5 * jnp.dot(a, wd_ref[...], preferred_element_type=F32)


def _ffn_kernel(x_ref, gain_ref, wgu_ref, wd_ref, o_ref):
    x = x_ref[...]
    g, up = _gate_up(x, gain_ref[...], wgu_ref)
    o_ref[...] = _half_ffn_out(x, g, up, wd_ref)


def _ffn(x2d, gain, w_gu, w_down, *, tm):
    n, d = x2d.shape
    d_ff = w_down.shape[0]
    return pl.pallas_call(
        _ffn_kernel,
        out_shape=jax.ShapeDtypeStruct((n, d), F32),
        grid=(n // tm,),
        in_specs=[
            pl.BlockSpec((tm, d), lambda i: (i, 0)),
            _const_spec((1, d)),
            _const_spec((d, 2 * d_ff)),
            _const_spec((d_ff, d)),
        ],
        out_specs=pl.BlockSpec((tm, d), lambda i: (i, 0)),
        compiler_params=pltpu.CompilerParams(
            dimension_semantics=("arbitrary",), vmem_limit_bytes=V7X_VMEM_LIMIT_BYTES),
        name="ffn",
    )(x2d, gain, w_gu, w_down)


def _bf16_parts(c):
    hi = c.astype(BF16).astype(F32)
    r = c - hi
    mid = r.astype(BF16).astype(F32)
    lo = (r - mid).astype(BF16).astype(F32)
    return hi, mid, lo


def _rows8(vals, tm):
    row = lax.broadcasted_iota(jnp.int32, (8, tm), 0)
    out = jnp.zeros((8, tm), F32)
    for j, v in enumerate(vals):
        out = jnp.where(row == j, v, out)
    return out


def _key_tail(one_rows, first_row, tm):
    return jnp.concatenate([jnp.zeros((SHIFT_ROW - first_row, tm), F32), one_rows,
                            jnp.zeros((HEAD_PAD - SHIFT_ROW - 8, tm), F32)], axis=0)


def _rms_rows(x, n):
    return lax.rsqrt(jnp.sum(x * x, axis=0, keepdims=True) * (1.0 / n) + EPS)


def _rope_rows(x, cos, sin):
    x1 = x[:MLA_ROPE // 2]
    x2 = x[MLA_ROPE // 2:]
    return x1 * cos - x2 * sin, x1 * sin + x2 * cos


_ProjRefs = collections.namedtuple(
    "_ProjRefs", "gain wlowT wwideT bf bg gq gk gcq wuqT gckv wukvT gmq gmk cos sin qT k vT gT dg")


class _ProjOps:
    def __init__(self, u, r, tm, meta):
        self.u, self.r, self.tm, self.meta = u, r, tm, meta
        n_gate = r.wwideT.shape[0] - WIDE_GATE
        self.bounds = [WIDE_GATE + n_gate * e // 8 for e in GATE_SPLIT_EIGHTHS]

    def _wide_rows(self, r0, r1):
        return lax.dot_general(self.r.wwideT[r0:r1, :], self.u, _NT, preferred_element_type=F32)

    def low(self):
        return lax.dot_general(self.r.wlowT[...], self.u, _NT, preferred_element_type=F32)

    def fox(self):
        return self._wide_rows(0, WIDE_GATE)

    def gate_logits(self, j):
        return self._wide_rows(self.bounds[j], self.bounds[j + 1])

    def store_gates(self, j, logits):
        rows = slice(self.bounds[j] - WIDE_GATE, self.bounds[j + 1] - WIDE_GATE)
        bias = jnp.concatenate([self.r.bg[rows, :]] * (self.tm // 128), axis=1)
        self.r.gT[0, rows, :] = jax.nn.sigmoid(logits + bias).astype(BF16)

    def forget_cumsum(self, low, carry):
        tm = self.tm
        fl = low[LOW_FL:LOW_FL + FOX_HEADS] + self.r.bf[...]
        lf = jnp.minimum(fl, 0.0) - jnp.log1p(jnp.exp(-jnp.abs(fl)))
        r_i = lax.broadcasted_iota(jnp.int32, (tm, tm), 0)
        c_i = lax.broadcasted_iota(jnp.int32, (tm, tm), 1)
        upper = jnp.where(r_i <= c_i, 1.0, 0.0).astype(BF16)
        cs = None
        for part in _bf16_parts(lf):
            d = jnp.dot(part.astype(BF16), upper, preferred_element_type=F32)
            cs = d if cs is None else cs + d
        if self.meta:
            return (cs - cs[:, N_META - 1:N_META]) * LOG2E, None
        return (cs + carry) * LOG2E, carry + cs[:, tm - 1:tm]

    def mla_up(self, low):
        r = self.r
        cq = low[LOW_CQ:LOW_CQ + MLA_Q_RANK]
        cqn = (cq * _rms_rows(cq, MLA_Q_RANK) * r.gcq[...]).astype(BF16)
        qm = jnp.dot(r.wuqT[...], cqn, preferred_element_type=F32)
        ckv = low[LOW_CKV:LOW_CKV + MLA_KV_RANK]
        ckvn = (ckv * _rms_rows(ckv, MLA_KV_RANK) * r.gckv[...]).astype(BF16)
        kv = jnp.dot(r.wukvT[...], ckvn, preferred_element_type=F32)
        return qm, kv

    def store_mla_heads(self, low, qm, kv):
        r, tm = self.r, self.tm
        kr = low[LOW_KR:LOW_KR + MLA_ROPE]
        ss_kr = jnp.sum(kr * kr, axis=0, keepdims=True)
        cos = r.cos[...]
        sin = r.sin[...]
        gmq = r.gmq[...]
        gmk = r.gmk[...]
        zpad = jnp.zeros((HEAD_PAD - MLA_QK, tm), F32)
        ones = jnp.ones((1, tm), F32)
        ktail = _key_tail(_rows8([ones, ones, ones], tm), MLA_QK, tm)
        for hh in range(MLA_HEADS):
            qh = qm[hh * MLA_QK:(hh + 1) * MLA_QK]
            qn = qh * (_rms_rows(qh, MLA_QK) * (MLA_QK ** -0.5 * LOG2E)) * gmq
            q1, q2 = _rope_rows(qn[MLA_NOPE:], cos, sin)
            q_ext = jnp.concatenate([qn[:MLA_NOPE], q1, q2, zpad], axis=0)
            kn_raw = kv[hh * (MLA_NOPE + MLA_V):hh * (MLA_NOPE + MLA_V) + MLA_NOPE]
            rk = lax.rsqrt((jnp.sum(kn_raw * kn_raw, axis=0, keepdims=True) + ss_kr) * (1.0 / MLA_QK) + EPS)
            kn = kn_raw * rk * gmk[:MLA_NOPE]
            k1, k2 = _rope_rows(kr * rk * gmk[MLA_NOPE:], cos, sin)
            k_ext = jnp.concatenate([kn, k1, k2, ktail], axis=0)
            g = FOX_HEADS + hh
            r.qT[0, g * HEAD_PAD:(g + 1) * HEAD_PAD, :] = q_ext.astype(BF16)
            r.k[0, :, g * HEAD_PAD:(g + 1) * HEAD_PAD] = k_ext.T.astype(BF16)
            r.dg[0, g] = jnp.sum(q_ext * k_ext, axis=0, keepdims=True)
            r.vT[0, g * V_DIM:(g + 1) * V_DIM, :] = (
                kv[hh * (MLA_NOPE + MLA_V) + MLA_NOPE:(hh + 1) * (MLA_NOPE + MLA_V)].astype(BF16))

    def store_fox_heads(self, fox, c):
        r, tm = self.r, self.tm
        ones = jnp.ones((1, tm), F32)
        one_rows = _rows8([ones, ones, ones], tm)
        zpad = jnp.zeros((HEAD_PAD - FOX_DIM - 16, tm), F32)
        ktail = _key_tail(one_rows, FOX_DIM + 16, tm)
        gq = r.gq[...]
        gk = r.gk[...]
        for hh in range(FOX_HEADS):
            qh = fox[WIDE_FQ + hh * FOX_DIM:WIDE_FQ + (hh + 1) * FOX_DIM]
            kh = fox[WIDE_FK + hh * FOX_DIM:WIDE_FK + (hh + 1) * FOX_DIM]
            qn = qh * (_rms_rows(qh, FOX_DIM) * (FOX_DIM ** -0.5 * LOG2E)) * gq
            kn = kh * _rms_rows(kh, FOX_DIM) * gk
            c_rows = _rows8(list(_bf16_parts(c[hh:hh + 1])), tm)
            q_ext = jnp.concatenate([qn, c_rows, one_rows, zpad], axis=0)
            k_ext = jnp.concatenate([kn, one_rows, -c_rows, ktail], axis=0)
            r.qT[0, hh * HEAD_PAD:(hh + 1) * HEAD_PAD, :] = q_ext.astype(BF16)
            r.k[0, :, hh * HEAD_PAD:(hh + 1) * HEAD_PAD] = k_ext.T.astype(BF16)
            r.dg[0, hh] = jnp.sum(q_ext * k_ext, axis=0, keepdims=True)
            r.vT[0, hh * V_DIM:(hh + 1) * V_DIM, :] = (
                fox[WIDE_FV + hh * FOX_DIM:WIDE_FV + (hh + 1) * FOX_DIM].astype(BF16))


def _rms_norm_bf16(x, gain):
    ms = jnp.mean(x * x, axis=-1, keepdims=True)
    return (x * lax.rsqrt(ms + EPS) * gain).astype(BF16)


def _proj_meta_kernel(h_ref, *refs):
    r = _ProjRefs(*refs)
    tm = h_ref.shape[1]
    ops = _ProjOps(_rms_norm_bf16(h_ref[0], r.gain[...]), r, tm, meta=True)
    low = ops.low()
    fox = ops.fox()
    c, _ = ops.forget_cumsum(low, None)
    qm, kv = ops.mla_up(low)
    ops.store_fox_heads(fox, c)
    ops.store_mla_heads(low, qm, kv)
    for j in range(len(ops.bounds) - 1):
        ops.store_gates(j, ops.gate_logits(j))


def _ffn_proj_kernel(x_ref, fgain_ref, wgu_ref, wd_ref, *refs, tiles_per_row):
    *proj_refs, h1_ref, qT_ref, k_ref, vT_ref, gT_ref, dg_ref, hprev_ref, carry_ref = refs
    r = _ProjRefs(*proj_refs, qT_ref, k_ref, vT_ref, gT_ref, dg_ref)
    t = pl.program_id(0)
    tm = x_ref.shape[0]

    @pl.when(t == 0)
    def _():
        hprev_ref[...] = jnp.zeros_like(hprev_ref)
        carry_ref[...] = jnp.zeros_like(carry_ref)

    ops = _ProjOps(_rms_norm_bf16(hprev_ref[...], r.gain[...]), r, tm, meta=False)
    first_of_row = lax.rem(t + (tiles_per_row - 1), tiles_per_row) == 0
    carry = jnp.where(first_of_row, 0.0, carry_ref[:, 0:1])
    x = x_ref[...]

    low = ops.low()
    g, up = _gate_up(x, fgain_ref[...], wgu_ref)
    c, carry = ops.forget_cumsum(low, carry)
    qm, kv = ops.mla_up(low)
    fox = ops.fox()
    ops.store_mla_heads(low, qm, kv)
    h1 = _half_ffn_out(x, g, up, wd_ref)
    ops.store_fox_heads(fox, c)
    n_chunks = len(ops.bounds) - 1
    gate = ops.gate_logits(0)
    for j in range(n_chunks):
        gate_next = ops.gate_logits(j + 1) if j + 1 < n_chunks else None
        ops.store_gates(j, gate)
        gate = gate_next
    h1_ref[...] = h1
    hprev_ref[...] = h1
    carry_ref[...] = jnp.broadcast_to(carry, carry_ref.shape)


def _proj_consts(p):
    return [p["mix_norm"], p["w_lowT"], p["w_wideT"], p["b_forget"], p["b_gate"], p["fox_q_norm"],
            p["fox_k_norm"], p["mla_cq_norm"], p["mla_w_uqT"], p["mla_ckv_norm"], p["mla_w_ukvT"],
            p["mla_q_norm"], p["mla_k_norm"]]


def _proj_out_shapes(b, s, n_gate):
    return (
        jax.ShapeDtypeStruct((b, N_HEADS * HEAD_PAD, s), BF16),
        jax.ShapeDtypeStruct((b, s, N_HEADS * HEAD_PAD), BF16),
        jax.ShapeDtypeStruct((b, N_HEADS * V_DIM, s), BF16),
        jax.ShapeDtypeStruct((b, n_gate, s), BF16),
        jax.ShapeDtypeStruct((b, N_HEADS, 1, s), F32),
    )


def _proj_meta(h3d, cosT, sinT, p):
    _, tm, d = h3d.shape
    n_gate = p["w_wideT"].shape[0] - WIDE_GATE
    consts = _proj_consts(p)
    whole = lambda shape: pl.BlockSpec(shape, lambda i: (0,) * len(shape))
    return pl.pallas_call(
        _proj_meta_kernel,
        out_shape=_proj_out_shapes(1, tm, n_gate),
        grid=(1,),
        in_specs=[whole(h3d.shape)] + [whole(c.shape) for c in consts] + [whole(cosT.shape), whole(sinT.shape)],
        out_specs=tuple(whole(o.shape) for o in _proj_out_shapes(1, tm, n_gate)),
        compiler_params=pltpu.CompilerParams(
            dimension_semantics=("arbitrary",), vmem_limit_bytes=V7X_VMEM_LIMIT_BYTES),
        name="proj_meta",
    )(h3d, *consts, cosT, sinT)


def _ffn_proj(x2d, fgain, w_gu, w_down, cosT, sinT, p, *, b, s, tm):
    n, d = x2d.shape
    n_gate = p["w_wideT"].shape[0] - WIDE_GATE
    consts = _proj_consts(p)
    tiles_per_row = s // tm
    n_tiles = n // tm

    def cur(t):
        return jnp.minimum(t, n_tiles - 1)

    def prev_row(t):
        tt = jnp.maximum(t - 1, 0)
        return tt // tiles_per_row, tt % tiles_per_row

    outs = pl.pallas_call(
        functools.partial(_ffn_proj_kernel, tiles_per_row=tiles_per_row),
        out_shape=(jax.ShapeDtypeStruct((n, d), F32),) + _proj_out_shapes(b, s, n_gate),
        grid=(n_tiles + 1,),
        in_specs=[
            pl.BlockSpec((tm, d), lambda t: (cur(t), 0)),
            _const_spec(fgain.shape),
            _const_spec(w_gu.shape),
            _const_spec(w_down.shape),
        ] + [_const_spec(c.shape) for c in consts]
        + [pl.BlockSpec((MLA_ROPE // 2, tm), lambda t: (0, prev_row(t)[1]))] * 2,
        out_specs=(
            pl.BlockSpec((tm, d), lambda t: (cur(t), 0)),
            pl.BlockSpec((1, N_HEADS * HEAD_PAD, tm), lambda t: (prev_row(t)[0], 0, prev_row(t)[1])),
            pl.BlockSpec((1, tm, N_HEADS * HEAD_PAD), lambda t: (prev_row(t)[0], prev_row(t)[1], 0)),
            pl.BlockSpec((1, N_HEADS * V_DIM, tm), lambda t: (prev_row(t)[0], 0, prev_row(t)[1])),
            pl.BlockSpec((1, n_gate, tm), lambda t: (prev_row(t)[0], 0, prev_row(t)[1])),
            pl.BlockSpec((1, N_HEADS, 1, tm), lambda t: (prev_row(t)[0], 0, 0, prev_row(t)[1])),
        ),
        scratch_shapes=[pltpu.VMEM((tm, d), F32), pltpu.VMEM((FOX_HEADS, 128), F32)],
        compiler_params=pltpu.CompilerParams(
            dimension_semantics=("arbitrary",), vmem_limit_bytes=V7X_VMEM_LIMIT_BYTES),
        name="ffn_proj",
    )(x2d, fgain, w_gu, w_down, *consts, cosT, sinT)
    return outs


def _attn_kernel(qT_ref, k_ref, vT_ref, dg_ref, km_ref, vmT_ref, o_ref, *, tq, group):
    s_len = qT_ref.shape[2]
    row = lax.broadcasted_iota(jnp.int32, (tq, tq), 0)
    col = lax.broadcasted_iota(jnp.int32, (tq, tq), 1)
    causal = row <= col
    both = lax.broadcasted_iota(jnp.int32, (tq, 2 * tq), 1)
    row2 = lax.broadcasted_iota(jnp.int32, (tq, 2 * tq), 0)
    causal_pair = (row2 <= both) | (both >= tq)
    ones_real = jnp.ones((16, s_len), BF16)
    ones_meta = jnp.ones((16, N_META), BF16)
    v_ext = [jnp.concatenate([vT_ref[0, g * V_DIM:(g + 1) * V_DIM, :], ones_real], axis=0)
             for g in range(group)]
    vm_ext = [jnp.concatenate([vmT_ref[g * V_DIM:(g + 1) * V_DIM, :], ones_meta], axis=0)
              for g in range(group)]
    q_tail_zeros = jnp.zeros((HEAD_PAD - SHIFT_ROW - 8, 2 * tq), F32)
    q_head_zeros = jnp.zeros((SHIFT_ROW - MLA_QK, 2 * tq), F32)

    def meta_scores(i, g, exact):
        head = slice(g * HEAD_PAD, (g + 1) * HEAD_PAD)
        qT = qT_ref[0, head, 2 * i * tq:(2 * i + 2) * tq]
        sm = jnp.dot(km_ref[:, head], qT, preferred_element_type=F32)
        shift = None
        if not exact:
            own = dg_ref[0, g, :, 2 * i * tq:(2 * i + 2) * tq]
            shift = jnp.maximum(jnp.max(sm, axis=0, keepdims=True), own)
            parts = _rows8([-part for part in _bf16_parts(shift)], 2 * tq)
            tail = jnp.concatenate([q_head_zeros, parts, q_tail_zeros], axis=0).astype(BF16)
            qT = jnp.concatenate([qT[:MLA_QK], tail], axis=0)
        return qT, sm, shift

    def scores(i, g, qT, sm, shift):
        head = slice(g * HEAD_PAD, (g + 1) * HEAD_PAD)
        n1 = (2 * i + 1) * tq
        s_main = jnp.dot(k_ref[0, :n1, head], qT, preferred_element_type=F32)
        s_low = jnp.dot(k_ref[0, n1:n1 + tq, head], qT[:, tq:], preferred_element_type=F32)
        return s_main, s_low, sm, shift

    def finish(i, g, s_main, s_low, sm, shift, exact):
        n1 = (2 * i + 1) * tq
        diag = jnp.where(causal_pair, s_main[n1 - tq:], MASK_VALUE)
        low = jnp.where(causal, s_low, MASK_VALUE)
        top = s_main[:n1 - tq] if i > 0 else None
        if exact:
            m = jnp.maximum(jnp.max(diag, axis=0, keepdims=True), jnp.max(sm, axis=0, keepdims=True))
            if i > 0:
                m = jnp.maximum(m, jnp.max(top, axis=0, keepdims=True))
            m = jnp.concatenate([m[:, :tq], jnp.maximum(m[:, tq:], jnp.max(low, axis=0, keepdims=True))],
                                axis=1)
            diag, low, sm = diag - m, low - m[:, tq:], sm - m
            top = top - m if i > 0 else None
        else:
            sm = sm - shift
        parts = [jnp.exp2(diag).astype(BF16)]
        if i > 0:
            parts = [jnp.exp2(top).astype(BF16)] + parts
        p = jnp.concatenate(parts, axis=0) if len(parts) > 1 else parts[0]
        p_low = jnp.exp2(low).astype(BF16)
        pm = jnp.exp2(sm).astype(BF16)
        o = (jnp.dot(v_ext[g][:, :n1], p, preferred_element_type=F32)
             + jnp.dot(vm_ext[g], pm, preferred_element_type=F32))
        o_hi = o[:, tq:] + jnp.dot(v_ext[g][:, n1:n1 + tq], p_low, preferred_element_type=F32)
        den = jnp.concatenate([o[V_DIM:V_DIM + 1, :tq], o_hi[V_DIM:V_DIM + 1]], axis=1)
        rows = slice(g * V_DIM, (g + 1) * V_DIM)
        o_ref[0, rows, 2 * i * tq:(2 * i + 1) * tq] = (o[:V_DIM, :tq] * (1.0 / den[:, :tq])).astype(BF16)
        o_ref[0, rows, (2 * i + 1) * tq:(2 * i + 2) * tq] = (o_hi[:V_DIM] * (1.0 / den[:, tq:])).astype(BF16)
        usable = (den > 0.0) & (den < jnp.inf)
        return jnp.where(usable, 0.0, 1.0)

    nq = s_len // (2 * tq)
    order = [t for pair in zip(range(nq - 1, -1, -1), range(nq)) for t in pair][:nq]

    def run(exact):
        flags = jnp.zeros((1, 2 * tq), F32)
        meta = {(i, g): meta_scores(i, g, exact) for i in order for g in range(group)}
        pending = []
        for i in order:
            for g in range(group):
                pending.append((i, g) + scores(i, g, *meta[i, g]))
                if len(pending) > ATTN_LOOKAHEAD:
                    flags = jnp.maximum(flags, finish(*pending.pop(0), exact))
        for unit in pending:
            flags = jnp.maximum(flags, finish(*unit, exact))
        return flags

    flags = run(exact=False)

    @pl.when(jnp.max(flags) > 0.0)
    def _():
        run(exact=True)


def _attention(qT, k, vT, dg, k_meta, vT_meta, *, tq, group):
    b, _, s = qT.shape
    return pl.pallas_call(
        functools.partial(_attn_kernel, tq=tq, group=group),
        out_shape=jax.ShapeDtypeStruct((b, N_HEADS * V_DIM, s), BF16),
        grid=(b, N_HEADS // group),
        in_specs=[
            pl.BlockSpec((1, group * HEAD_PAD, s), lambda i, h: (i, h, 0)),
            pl.BlockSpec((1, s, group * HEAD_PAD), lambda i, h: (i, 0, h)),
            pl.BlockSpec((1, group * V_DIM, s), lambda i, h: (i, h, 0)),
            pl.BlockSpec((1, group, 1, s), lambda i, h: (i, h, 0, 0)),
            pl.BlockSpec((N_META, group * HEAD_PAD), lambda i, h: (0, h)),
            pl.BlockSpec((group * V_DIM, N_META), lambda i, h: (h, 0)),
        ],
        out_specs=pl.BlockSpec((1, group * V_DIM, s), lambda i, h: (i, h, 0)),
        compiler_params=pltpu.CompilerParams(
            dimension_semantics=("arbitrary", "arbitrary"), vmem_limit_bytes=V7X_VMEM_LIMIT_BYTES),
        name="attention",
    )(qT, k, vT, dg, k_meta, vT_meta)


def _out_ffn_kernel(oT_ref, gT_ref, h_ref, wbfT_ref, wbmT_ref, woT_ref, gain_ref, wgu_ref, wd_ref, o_ref, *, sub):
    tm = h_ref.shape[1]
    d = woT_ref.shape[0]
    n_sub = tm // sub

    def branches(j):
        lanes = slice(j * sub, (j + 1) * sub)
        yf = jnp.dot(wbfT_ref[...], oT_ref[0, :FOX_W, lanes], preferred_element_type=F32)
        ym = jnp.dot(wbmT_ref[...], oT_ref[0, FOX_W:, lanes], preferred_element_type=F32)
        gf = gT_ref[0, :d, lanes].astype(F32)
        gm = gT_ref[0, d:, lanes].astype(F32)
        return (gf * yf + gm * ym).astype(BF16)

    def mixed(j, z):
        mix = lax.dot_general(z, woT_ref[...], _TN, preferred_element_type=F32)
        return h_ref[0, j * sub:(j + 1) * sub, :] + mix

    zs = [branches(j) for j in range(n_sub)]
    h2 = [mixed(j, zs[j]) for j in range(n_sub)]
    gu = [_gate_up(h2[j], gain_ref[...], wgu_ref) for j in range(n_sub)]
    for j in range(n_sub):
        o_ref[0, j * sub:(j + 1) * sub, :] = _half_ffn_out(h2[j], *gu[j], wd_ref)


def _out_ffn(oT, gT, h3d, wbfT, wbmT, woT, gain, w_gu, w_down, *, tm, sub):
    b, s, d = h3d.shape
    return pl.pallas_call(
        functools.partial(_out_ffn_kernel, sub=sub),
        out_shape=jax.ShapeDtypeStruct((b, s, d), F32),
        grid=(b, s // tm),
        in_specs=[
            pl.BlockSpec((1, oT.shape[1], tm), lambda i, t: (i, 0, t)),
            pl.BlockSpec((1, gT.shape[1], tm), lambda i, t: (i, 0, t)),
            pl.BlockSpec((1, tm, d), lambda i, t: (i, t, 0)),
            _const_spec(wbfT.shape),
            _const_spec(wbmT.shape),
            _const_spec(woT.shape),
            _const_spec(gain.shape),
            _const_spec(w_gu.shape),
            _const_spec(w_down.shape),
        ],
        out_specs=pl.BlockSpec((1, tm, d), lambda i, t: (i, t, 0)),
        compiler_params=pltpu.CompilerParams(
            dimension_semantics=("arbitrary", "arbitrary"), vmem_limit_bytes=V7X_VMEM_LIMIT_BYTES),
        name="out_ffn",
    )(oT, gT, h3d, wbfT, wbmT, woT, gain, w_gu, w_down)


def _col(v):
    return v.astype(F32).reshape(-1, 1)


def _pick_tile(n, pref):
    t = min(n, pref)
    while n % t:
        t //= 2
    return t


def kernel(x, meta_tokens, ffn1_norm, ffn1_w_gu, ffn1_w_down, mix_norm, w_in, b_forget, b_gate, fox_q_norm,
           fox_k_norm, mla_cq_norm, mla_w_uq, mla_ckv_norm, mla_w_ukv, mla_q_norm, mla_k_norm, w_branch_fox,
           w_branch_mla, w_out, ffn2_norm, ffn2_w_gu, ffn2_w_down):
    b, s, d = x.shape
    depth = ffn1_norm.shape[0]
    if depth != 1:
        raise NotImplementedError("only depth 1 is supported")
    meta_rows = 128

    pos = jnp.arange(N_META + s, dtype=F32)
    inv_freq = ROPE_THETA ** (-jnp.arange(0, MLA_ROPE, 2, dtype=F32) / MLA_ROPE)
    ang = pos[:, None] * inv_freq[None, :]
    cosT, sinT = jnp.cos(ang).T, jnp.sin(ang).T
    pad = ((0, 0), (0, meta_rows - N_META))
    cos_meta, sin_meta = jnp.pad(cosT[:, :N_META], pad), jnp.pad(sinT[:, :N_META], pad)
    cos_real, sin_real = cosT[:, N_META:], sinT[:, N_META:]

    tm_fused = _pick_tile(s, 256)
    tm_out = _pick_tile(s, 512)
    sub_out = _pick_tile(tm_out, 256)
    tq = _pick_tile(s, 256)

    w1gu, w1d = ffn1_w_gu[0].astype(BF16), ffn1_w_down[0].astype(BF16)
    w2gu, w2d = ffn2_w_gu[0].astype(BF16), ffn2_w_down[0].astype(BF16)
    g1, g2 = ffn1_norm[0].reshape(1, d).astype(F32), ffn2_norm[0].reshape(1, d).astype(F32)
    p = {
        "mix_norm": mix_norm[0].reshape(1, d).astype(F32),
        "w_lowT": w_in[0][:, OFF_FL:OFF_GATE].T.astype(BF16),
        "w_wideT": jnp.concatenate([w_in[0][:, :OFF_FL], w_in[0][:, OFF_GATE:]], axis=1).T.astype(BF16),
        "b_forget": _col(b_forget[0]),
        "b_gate": jnp.broadcast_to(_col(b_gate[0]), (b_gate.shape[1], 128)),
        "fox_q_norm": _col(fox_q_norm[0]),
        "fox_k_norm": _col(fox_k_norm[0]),
        "mla_cq_norm": _col(mla_cq_norm[0]),
        "mla_w_uqT": mla_w_uq[0].T.astype(BF16),
        "mla_ckv_norm": _col(mla_ckv_norm[0]),
        "mla_w_ukvT": mla_w_ukv[0].T.astype(BF16),
        "mla_q_norm": _col(mla_q_norm[0]),
        "mla_k_norm": _col(mla_k_norm[0]),
    }
    wbfT = w_branch_fox[0].T.astype(BF16)
    wbmT = w_branch_mla[0].T.astype(BF16)
    woT = w_out[0].astype(BF16)

    hm = jnp.pad(meta_tokens.astype(F32), ((0, meta_rows - N_META), (0, 0)))
    hm1 = _ffn(hm, g1, w1gu, w1d, tm=meta_rows)
    _, km, vmT, _, _ = _proj_meta(hm1[None], cos_meta, sin_meta, p)
    k_meta, vT_meta = km[0, :N_META], vmT[0, :, :N_META]

    h1, qT, k, vT, gT, dg = _ffn_proj(x.astype(F32).reshape(b * s, d), g1, w1gu, w1d, cos_real, sin_real, p,
                                  b=b, s=s, tm=tm_fused)
    h1 = h1.reshape(b, s, d)
    oT = _attention(qT, k, vT, dg, k_meta, vT_meta, tq=tq, group=ATTN_GROUP)
    h3 = _out_ffn(oT, gT, h1, wbfT, wbmT, woT, g2, w2gu, w2d, tm=tm_out, sub=sub_out)
    return h3.astype(x.dtype)
```
